```python
import functools
import jax, jax.numpy as jnp
from jax import lax
import numpy as np

D_MODEL = 1024
BATCH = 8
SEQ = 2048
DEPTH = 1
DEC_BATCH = 128
DEC_SEQ = 8
PAST_LEN = 8192
PAGE_SIZE = 128

C_CONV = D_MODEL
CONV_W = 31
N_HEADS = 8
QK_NOPE = 128
QK_ROPE = 64
V_HEAD = D_MODEL // N_HEADS
Q_LORA = 512
KV_LORA = 512
ROPE_THETA = 10000.0
ATTN_SCALE = (QK_NOPE + QK_ROPE) ** -0.5
Q_BLOCK = 128
N_EXPERTS = 32
TOP_K = 4
D_EXPERT = D_MODEL
SWIGLU_LIMIT = 7.0
SWIGLU_ALPHA = 1.702
PLE_DIM = 256
EPS = 1e-6
NEG = -1e30
IN_SPLITS = (2 * C_CONV, Q_LORA, KV_LORA, QK_ROPE, D_MODEL, D_MODEL)
IN_DIM = sum(IN_SPLITS)

kernel_name = 'hybrid_conv_mla_moe_step'


def rms_norm(x, g):
    xf = x.astype(jnp.float32)
    y = xf * lax.rsqrt(jnp.mean(jnp.square(xf), axis=-1, keepdims=True) + EPS)
    return (y * g.astype(jnp.float32)).astype(x.dtype)


def layer_norm(x, g, b):
    xf = x.astype(jnp.float32)
    mu = jnp.mean(xf, axis=-1, keepdims=True)
    var = jnp.mean(jnp.square(xf - mu), axis=-1, keepdims=True)
    y = (xf - mu) * lax.rsqrt(var + EPS)
    return (y * g.astype(jnp.float32) + b.astype(jnp.float32)).astype(x.dtype)


def rope(x, pos):
    half = QK_ROPE // 2
    inv_freq = ROPE_THETA ** (-jnp.arange(half, dtype=jnp.float32) / half)
    ang = pos.astype(jnp.float32)[:, None] * inv_freq[None, :]
    ang = ang.reshape((ang.shape[0],) + (1,) * (x.ndim - 3) + (half,))
    cos, sin = jnp.cos(ang), jnp.sin(ang)
    xf = x.astype(jnp.float32)
    x1, x2 = xf[..., :half], xf[..., half:]
    return jnp.concatenate([x1 * cos - x2 * sin, x2 * cos + x1 * sin], axis=-1).astype(x.dtype)


def conv_branch(u, prefix, w_dw, b_dw, ln_g, ln_b, w_pw2, b_pw2):
    a, gate = jnp.split(u, 2, axis=-1)
    v = a * jax.nn.sigmoid(gate)
    v_ext = jnp.concatenate([prefix.astype(v.dtype), v], axis=1)
    c = lax.conv_general_dilated(
        v_ext, w_dw[:, None, :].astype(v.dtype), window_strides=(1,), padding='VALID',
        dimension_numbers=('NWC', 'WIO', 'NWC'), feature_group_count=C_CONV) + b_dw
    out = jax.nn.silu(layer_norm(c, ln_g, ln_b)) @ w_pw2 + b_pw2
    return out, v_ext[:, -(CONV_W - 1):]


def prompt_attention(q_lat, q_pe, c_kv, k_pe):
    b, t = q_lat.shape[:2]
    nb = t // Q_BLOCK
    to_blocks = lambda z: jnp.moveaxis(z.reshape((b, nb, Q_BLOCK) + z.shape[2:]), 1, 0)
    k_pos = jnp.arange(t)

    def one_block(args):
        ql, qp, blk = args
        s = (jnp.einsum('bqhl,bkl->bhqk', ql, c_kv)
             + jnp.einsum('bqhr,bkr->bhqk', qp, k_pe)).astype(jnp.float32) * ATTN_SCALE
        q_pos = blk * Q_BLOCK + jnp.arange(Q_BLOCK)
        s = jnp.where(k_pos[None, :] <= q_pos[:, None], s, NEG)
        p = jax.nn.softmax(s, axis=-1).astype(c_kv.dtype)
        return jnp.einsum('bhqk,bkl->bqhl', p, c_kv)

    o = lax.map(one_block, (to_blocks(q_lat), to_blocks(q_pe), jnp.arange(nb)))
    return jnp.moveaxis(o, 0, 1).reshape(b, t, N_HEADS, KV_LORA)


def sample_attention(q_lat, q_pe, c_kv, k_pe, cache_ckv, cache_kpe, page_table):
    b, s_len = q_lat.shape[:2]
    f32 = jnp.float32

    def accumulate(carry, s, ck):
        m, l, acc = carry
        m_new = jnp.maximum(m, jnp.max(s, axis=-1))
        corr = jnp.exp(m - m_new)
        p = jnp.exp(s - m_new[..., None])
        l = l * corr + jnp.sum(p, axis=-1)
        acc = acc * corr[..., None] + jnp.einsum('bhsp,bpl->bhsl', p, ck.astype(f32))
        return (m_new, l, acc)

    def page_step(carry, pages):
        ck = cache_ckv[pages].astype(q_lat.dtype)
        kp = cache_kpe[pages].astype(q_pe.dtype)
        s = (jnp.einsum('bshl,bpl->bhsp', q_lat, ck)
             + jnp.einsum('bshr,bpr->bhsp', q_pe, kp)).astype(f32) * ATTN_SCALE
        return accumulate(carry, s, ck), None

    init = (jnp.full((b, N_HEADS, s_len), NEG, f32),
            jnp.zeros((b, N_HEADS, s_len), f32),
            jnp.zeros((b, N_HEADS, s_len, KV_LORA), f32))
    carry, _ = lax.scan(page_step, init, page_table.T)
    s = (jnp.einsum('bshl,bpl->bhsp', q_lat, c_kv)
         + jnp.einsum('bshr,bpr->bhsp', q_pe, k_pe)).astype(f32) * ATTN_SCALE
    causal = jnp.tril(jnp.ones((s_len, s_len), dtype=bool))
    s = jnp.where(causal, s, NEG)
    _, l, acc = accumulate(carry, s, c_kv)
    o = acc / l[..., None]
    return jnp.moveaxis(o, 1, 2).astype(q_lat.dtype)


def moe(h, w_router, b_router, w_e1, b_e1, w_e2, b_e2):
    shape = h.shape
    hf = h.reshape(-1, D_MODEL)
    logits = (hf @ w_router + b_router).astype(jnp.float32)
    top_v, top_i = lax.top_k(logits, TOP_K)
    gates = jax.nn.softmax(top_v, axis=-1)
    dense_gate = jnp.sum(jax.nn.one_hot(top_i, N_EXPERTS, dtype=jnp.float32) * gates[..., None],
                         axis=1).astype(h.dtype)
    out = jnp.zeros_like(hf)
    for e in range(N_EXPERTS):
        u = hf @ w_e1[e] + b_e1[e]
        glu = jnp.minimum(u[:, 0::2], SWIGLU_LIMIT)
        lin = jnp.clip(u[:, 1::2], -SWIGLU_LIMIT, SWIGLU_LIMIT)
        act = glu * jax.nn.sigmoid(SWIGLU_ALPHA * glu) * (lin + 1.0)
        out = out + dense_gate[:, e:e + 1] * (act @ w_e2[e] + b_e2[e])
    return out.reshape(shape)


def decoder_layer(x, p, pos, conv_prefix, attend,
                  norm_mix, w_in, w_dw, b_dw, ln_conv_g, ln_conv_b, w_pw2, b_pw2,
                  norm_q, w_qb, norm_kv, w_kvb, w_o,
                  norm_ffn, w_router, b_router, w_e1, b_e1, w_e2, b_e2,
                  norm_ple, w_ple_gate, w_ple_proj):
    b, t, _ = x.shape
    h = rms_norm(x, norm_mix)
    cuts = np.cumsum(IN_SPLITS)[:-1].tolist()
    u_conv, q_c, kv_c, k_pe_raw, z_conv, z_attn = jnp.split(h @ w_in, cuts, axis=-1)
    conv_out, conv_state = conv_branch(u_conv, conv_prefix, w_dw, b_dw, ln_conv_g, ln_conv_b, w_pw2, b_pw2)
    q = (rms_norm(q_c, norm_q) @ w_qb).reshape(b, t, N_HEADS, QK_NOPE + QK_ROPE)
    q_nope, q_pe = q[..., :QK_NOPE], rope(q[..., QK_NOPE:], pos)
    c_kv = rms_norm(kv_c, norm_kv)
    k_pe = rope(k_pe_raw, pos)
    w_kvb_h = w_kvb.reshape(KV_LORA, N_HEADS, QK_NOPE + V_HEAD)
    w_uk, w_uv = w_kvb_h[..., :QK_NOPE], w_kvb_h[..., QK_NOPE:]
    q_lat = jnp.einsum('bthn,lhn->bthl', q_nope, w_uk)
    o_lat = attend(q_lat, q_pe, c_kv, k_pe)
    attn_out = jnp.einsum('bthl,lhv->bthv', o_lat, w_uv).reshape(b, t, N_HEADS * V_HEAD)
    merged = jax.nn.sigmoid(z_conv) * conv_out + jax.nn.sigmoid(z_attn) * attn_out
    x = x + merged @ w_o
    x = x + moe(rms_norm(x, norm_ffn), w_router, b_router, w_e1, b_e1, w_e2, b_e2)
    x = x + jax.nn.sigmoid(rms_norm(x, norm_ple) @ w_ple_gate) * (p @ w_ple_proj)
    return x, c_kv, k_pe, conv_state


def setup_inputs(seed: int = 0) -> dict:
    key = jax.random.key(seed)
    ks = iter(jax.random.split(key, 40))
    f32 = jnp.float32

    def nrm(shape, scale):
        return scale * jax.random.normal(next(ks), shape, f32)

    def gain(shape):
        return 1.0 + 0.05 * jax.random.normal(next(ks), shape, f32)

    n_pages = PAST_LEN // PAGE_SIZE
    n_used = DEC_BATCH * n_pages
    n_phys = n_used + n_used // 4
    perm = jax.random.permutation(next(ks), n_phys)
    page_table = perm[:n_used].reshape(DEC_BATCH, n_pages).astype(jnp.int32)
    L = DEPTH
    return {
        'x_prompt': nrm((BATCH, SEQ, D_MODEL), 1.0),
        'x_sample': nrm((DEC_BATCH, DEC_SEQ, D_MODEL), 1.0),
        'cache_ckv': nrm((L, n_phys, PAGE_SIZE, KV_LORA), 1.0),
        'cache_kpe': nrm((L, n_phys, PAGE_SIZE, QK_ROPE), 1.0),
        'state_conv': nrm((L, DEC_BATCH, CONV_W - 1, C_CONV), 0.5),
        'page_table': page_table,
        'p_prompt': nrm((L, BATCH, SEQ, PLE_DIM), 1.0),
        'p_sample': nrm((L, DEC_BATCH, DEC_SEQ, PLE_DIM), 1.0),
        'norm_mix': gain((L, D_MODEL)),
        'w_in': nrm((L, D_MODEL, IN_DIM), D_MODEL ** -0.5),
        'w_dw': nrm((L, CONV_W, C_CONV), CONV_W ** -0.5),
        'b_dw': nrm((L, C_CONV), 0.02),
        'ln_conv_g': gain((L, C_CONV)),
        'ln_conv_b': nrm((L, C_CONV), 0.02),
        'w_pw2': nrm((L, C_CONV, D_MODEL), C_CONV ** -0.5),
        'b_pw2': nrm((L, D_MODEL), 0.02),
        'norm_q': gain((L, Q_LORA)),
        'w_qb': nrm((L, Q_LORA, N_HEADS * (QK_NOPE + QK_ROPE)), Q_LORA ** -0.5),
        'norm_kv': gain((L, KV_LORA)),
        'w_kvb': nrm((L, KV_LORA, N_HEADS * (QK_NOPE + V_HEAD)), KV_LORA ** -0.5),
        'w_o': nrm((L, D_MODEL, D_MODEL), D_MODEL ** -0.5),
        'norm_ffn': gain((L, D_MODEL)),
        'w_router': nrm((L, D_MODEL, N_EXPERTS), D_MODEL ** -0.5),
        'b_router': nrm((L, N_EXPERTS), 0.01),
        'w_e1': nrm((L, N_EXPERTS, D_MODEL, 2 * D_EXPERT), D_MODEL ** -0.5),
        'b_e1': nrm((L, N_EXPERTS, 2 * D_EXPERT), 0.02),
        'w_e2': nrm((L, N_EXPERTS, D_EXPERT, D_MODEL), D_EXPERT ** -0.5),
        'b_e2': nrm((L, N_EXPERTS, D_MODEL), 0.02),
        'norm_ple': gain((L, D_MODEL)),
        'w_ple_gate': nrm((L, D_MODEL, D_MODEL), D_MODEL ** -0.5),
        'w_ple_proj': nrm((L, PLE_DIM, D_MODEL), PLE_DIM ** -0.5),
        'norm_final': gain((D_MODEL,)),
    }


def reference(x_prompt, x_sample, cache_ckv, cache_kpe, state_conv, page_table, p_prompt, p_sample,
              norm_mix, w_in, w_dw, b_dw, ln_conv_g, ln_conv_b, w_pw2, b_pw2,
              norm_q, w_qb, norm_kv, w_kvb, w_o,
              norm_ffn, w_router, b_router, w_e1, b_e1, w_e2, b_e2,
              norm_ple, w_ple_gate, w_ple_proj, norm_final):
    past_len = page_table.shape[1] * PAGE_SIZE
    pos_prompt = jnp.arange(x_prompt.shape[1])
    pos_sample = past_len + jnp.arange(x_sample.shape[1])
    xp, xs = x_prompt, x_sample
    ckv_p, kpe_p, conv_p, ckv_s, kpe_s, conv_s = [], [], [], [], [], []
    for i in range(DEPTH):
        lw = (norm_mix[i], w_in[i], w_dw[i], b_dw[i], ln_conv_g[i], ln_conv_b[i], w_pw2[i], b_pw2[i],
              norm_q[i], w_qb[i], norm_kv[i], w_kvb[i], w_o[i],
              norm_ffn[i], w_router[i], b_router[i], w_e1[i], b_e1[i], w_e2[i], b_e2[i],
              norm_ple[i], w_ple_gate[i], w_ple_proj[i])
        zero_prefix = jnp.zeros((xp.shape[0], CONV_W - 1, C_CONV), xp.dtype)
        xp, c1, k1, s1 = decoder_layer(xp, p_prompt[i], pos_prompt, zero_prefix, prompt_attention, *lw)
        attend_s = functools.partial(sample_attention, cache_ckv=cache_ckv[i], cache_kpe=cache_kpe[i],
                                     page_table=page_table)
        xs, c2, k2, s2 = decoder_layer(xs, p_sample[i], pos_sample, state_conv[i], attend_s, *lw)
        ckv_p.append(c1); kpe_p.append(k1); conv_p.append(s1)
        ckv_s.append(c2); kpe_s.append(k2); conv_s.append(s2)
    y_prompt = rms_norm(xp, norm_final)
    y_sample = rms_norm(xs, norm_final)
    return (y_prompt, y_sample, jnp.stack(ckv_p), jnp.stack(kpe_p), jnp.stack(conv_p),
            jnp.stack(ckv_s), jnp.stack(kpe_s), jnp.stack(conv_s))
```

```python
import functools

import jax
import jax.numpy as jnp
from jax import lax
from jax.experimental import pallas as pl
from jax.experimental.pallas import tpu as pltpu

F32 = jnp.float32
BF16 = jnp.bfloat16

D_MODEL = 1024
N_HEADS = 8
QK_NOPE = 128
QK_ROPE = 64
V_HEAD = 128
Q_LORA = 512
KV_LORA = 512
N_EXPERTS = 32
TOP_K = 4
CONV_W = 31
ROPE_THETA = 10000.0
ATTN_SCALE = (QK_NOPE + QK_ROPE) ** -0.5
SWIGLU_LIMIT = 7.0
SWIGLU_ALPHA = 1.702
EPS = 1e-6
NEG = -1e30

LANES = 128
HALO = 32
EXPERT_TILE = 256
VMEM_LIMIT = 56 * 1024 * 1024

_C_GLU_A, _C_GLU_G, _C_Q, _C_KV, _C_KPE, _C_ZC, _C_ZA, _C_END = (
    0, 1024, 2048, 2560, 3072, 3200, 4224, 5248)


def _params(sem, **kw):
    return pltpu.CompilerParams(dimension_semantics=sem, vmem_limit_bytes=VMEM_LIMIT, **kw)


def _rms(x, g):
    return x * lax.rsqrt(jnp.mean(x * x, axis=-1, keepdims=True) + EPS) * g


def _dot(a, b):
    return jnp.dot(a, b, preferred_element_type=F32)


def _dot_nt(a, b):
    return lax.dot_general(a, b, (((1,), (1,)), ((), ())), preferred_element_type=F32)


def _rope128(x, cos, sin):
    lane = lax.broadcasted_iota(jnp.int32, x.shape, 1)
    rot = jnp.where((lane % QK_ROPE) < QK_ROPE // 2,
                    pltpu.roll(x, LANES - QK_ROPE // 2, 1), pltpu.roll(x, QK_ROPE // 2, 1))
    return x * cos + rot * sin


def _inproj_body(x_ref, cs_ref, nmix_ref, win_ref, nq_ref, wqb_ref, nkv_ref,
                 v_ref, q_ref, ckv_ref, kpe_ref, zc_ref, za_ref):
    h = _rms(x_ref[...], nmix_ref[...]).astype(BF16)

    def proj(lo, hi):
        return _dot(h, win_ref[:, lo:hi])

    v_ref[...] = proj(_C_GLU_A, _C_GLU_G) * jax.nn.sigmoid(proj(_C_GLU_G, _C_Q))
    qn = _rms(proj(_C_Q, _C_KV), nq_ref[...]).astype(BF16)
    ckv_ref[...] = _rms(proj(_C_KV, _C_KPE), nkv_ref[...])
    cos = cs_ref[:, 0:LANES]
    sin = cs_ref[:, LANES:2 * LANES]
    kpe_ref[...] = _rope128(proj(_C_KPE, _C_ZC), cos, sin)[:, :QK_ROPE]
    zc_ref[...] = jax.nn.sigmoid(proj(_C_ZC, _C_ZA))
    za_ref[...] = jax.nn.sigmoid(proj(_C_ZA, _C_END))
    q = _dot(qn, wqb_ref[...])
    nope_w = N_HEADS * QK_NOPE
    for j in range(N_HEADS // 2):
        pe = _rope128(q[:, nope_w + LANES * j:nope_w + LANES * (j + 1)], cos, sin)
        for hh in (2 * j, 2 * j + 1):
            q_ref[hh, :, 0:QK_NOPE] = q[:, QK_NOPE * hh:QK_NOPE * (hh + 1)].astype(BF16)
            lo = QK_ROPE * (hh % 2)
            q_ref[hh, :, QK_NOPE:QK_NOPE + QK_ROPE] = pe[:, lo:lo + QK_ROPE].astype(BF16)


def _inproj(x, cs, nmix, win, nq, wqb, nkv, tm):
    n = x.shape[0]
    wide = win.shape[1]
    row = lambda w: pl.BlockSpec((tm, w), lambda i: (i, 0))
    full = lambda a, b: pl.BlockSpec((a, b), lambda i: (0, 0))
    return pl.pallas_call(
        _inproj_body,
        grid=(n // tm,),
        in_specs=[row(D_MODEL), row(2 * LANES), full(1, D_MODEL), full(D_MODEL, wide),
                  full(1, Q_LORA), full(Q_LORA, wqb.shape[1]), full(1, KV_LORA)],
        out_specs=[row(D_MODEL),
                   pl.BlockSpec((N_HEADS, tm, QK_NOPE + QK_ROPE), lambda i: (0, i, 0)),
                   row(KV_LORA), row(QK_ROPE), row(D_MODEL), row(D_MODEL)],
        out_shape=[jax.ShapeDtypeStruct((n, D_MODEL), F32),
                   jax.ShapeDtypeStruct((N_HEADS, n, QK_NOPE + QK_ROPE), BF16),
                   jax.ShapeDtypeStruct((n, KV_LORA), F32),
                   jax.ShapeDtypeStruct((n, QK_ROPE), F32),
                   jax.ShapeDtypeStruct((n, D_MODEL), F32),
                   jax.ShapeDtypeStruct((n, D_MODEL), F32)],
        compiler_params=_params(("arbitrary",)),
        name="inproj",
    )(x, cs, nmix, win, nq, wqb, nkv)


def _kv_body(ckv_ref, kpe_ref, wkv_ref, k_ref, v_ref):
    kv = _dot(ckv_ref[...].astype(BF16), wkv_ref[...])
    kp = kpe_ref[...].astype(BF16)
    for hh in range(N_HEADS):
        k_ref[hh, :, 0:QK_NOPE] = kv[:, QK_NOPE * hh:QK_NOPE * (hh + 1)].astype(BF16)
        k_ref[hh, :, QK_NOPE:QK_NOPE + QK_ROPE] = kp
        lo = N_HEADS * QK_NOPE + V_HEAD * hh
        v_ref[hh] = kv[:, lo:lo + V_HEAD].astype(BF16)


def _prompt_kv(ckv, kpe, wkv, n_prompt, tm):
    return pl.pallas_call(
        _kv_body,
        grid=(n_prompt // tm,),
        in_specs=[pl.BlockSpec((tm, KV_LORA), lambda i: (i, 0)),
                  pl.BlockSpec((tm, QK_ROPE), lambda i: (i, 0)),
                  pl.BlockSpec(wkv.shape, lambda i: (0, 0))],
        out_specs=[pl.BlockSpec((N_HEADS, tm, QK_NOPE + QK_ROPE), lambda i: (0, i, 0)),
                   pl.BlockSpec((N_HEADS, tm, V_HEAD), lambda i: (0, i, 0))],
        out_shape=[jax.ShapeDtypeStruct((N_HEADS, n_prompt, QK_NOPE + QK_ROPE), BF16),
                   jax.ShapeDtypeStruct((N_HEADS, n_prompt, V_HEAD), BF16)],
        compiler_params=_params(("arbitrary",)),
        name="prompt_kv",
    )(ckv, kpe, wkv)


def _attn_body(q_ref, k_ref, v_ref, o_ref, *, tq):
    qi = pl.program_id(2)
    q = q_ref[0]
    row = lax.broadcasted_iota(jnp.int32, (tq, tq), 0)
    col = lax.broadcasted_iota(jnp.int32, (tq, tq), 1)

    def step(j, carry, diagonal):
        m, l, acc = carry
        off = pl.multiple_of(j * tq, tq)
        k = k_ref[0, pl.ds(off, tq), :]
        v = v_ref[0, pl.ds(off, tq), :]
        s = _dot_nt(q, k) * ATTN_SCALE
        if diagonal:
            s = jnp.where(col <= row, s, NEG)
        m_new = jnp.maximum(m, jnp.max(s, axis=-1, keepdims=True))
        corr = jnp.exp(m - m_new)
        p = jnp.exp(s - m_new)
        l = l * corr + jnp.sum(p, axis=-1, keepdims=True)
        acc = acc * corr + _dot(p.astype(BF16), v)
        return m_new, l, acc

    init = (jnp.full((tq, 1), NEG, F32), jnp.zeros((tq, 1), F32), jnp.zeros((tq, V_HEAD), F32))
    carry = lax.fori_loop(0, qi, lambda j, c: step(j, c, False), init)
    _, l, acc = step(qi, carry, True)
    o_ref[...] = acc / l


def _prompt_attention(q_cat, k_cat, v_h, batch, seq, tq):
    nq = seq // tq
    n_prompt = batch * seq
    return pl.pallas_call(
        functools.partial(_attn_body, tq=tq),
        grid=(batch, N_HEADS, nq),
        in_specs=[pl.BlockSpec((1, tq, QK_NOPE + QK_ROPE), lambda b, h, i: (h, b * nq + i, 0)),
                  pl.BlockSpec((1, seq, QK_NOPE + QK_ROPE), lambda b, h, i: (h, b, 0)),
                  pl.BlockSpec((1, seq, V_HEAD), lambda b, h, i: (h, b, 0))],
        out_specs=pl.BlockSpec((tq, V_HEAD), lambda b, h, i: (b * nq + i, h)),
        out_shape=jax.ShapeDtypeStruct((n_prompt, N_HEADS * V_HEAD), F32),
        compiler_params=_params(("arbitrary", "arbitrary", "arbitrary")),
        name="prompt_attention",
    )(q_cat, k_cat, v_h)


def _qabs_body(q_ref, wuk_ref, o_ref):
    q = q_ref[0]
    o_ref[0, :, 0:KV_LORA] = _dot(q[:, :QK_NOPE], wuk_ref[0])
    o_ref[0, :, KV_LORA:KV_LORA + QK_ROPE] = q[:, QK_NOPE:].astype(F32)


def _sample_q_absorb(q_cat, wuk_t, n_prompt, n_sample):
    blk = n_prompt // n_sample
    return pl.pallas_call(
        _qabs_body,
        grid=(N_HEADS,),
        in_specs=[pl.BlockSpec((1, n_sample, QK_NOPE + QK_ROPE), lambda h: (h, blk, 0)),
                  pl.BlockSpec((1, QK_NOPE, KV_LORA), lambda h: (h, 0, 0))],
        out_specs=pl.BlockSpec((1, n_sample, KV_LORA + QK_ROPE), lambda h: (h, 0, 0)),
        out_shape=jax.ShapeDtypeStruct((N_HEADS, n_sample, KV_LORA + QK_ROPE), F32),
        compiler_params=_params(("arbitrary",)),
        name="sample_q_absorb",
    )(q_cat, wuk_t)


def _oabs_body(o_ref, wuv_ref, out_ref):
    out_ref[...] = _dot(o_ref[0].astype(BF16), wuv_ref[0])


def _sample_o_expand(o_lat, wuv_h):
    n_sample = o_lat.shape[1]
    return pl.pallas_call(
        _oabs_body,
        grid=(N_HEADS,),
        in_specs=[pl.BlockSpec((1, n_sample, KV_LORA), lambda h: (h, 0, 0)),
                  pl.BlockSpec((1, KV_LORA, V_HEAD), lambda h: (h, 0, 0))],
        out_specs=pl.BlockSpec((n_sample, V_HEAD), lambda h: (0, h)),
        out_shape=jax.ShapeDtypeStruct((n_sample, N_HEADS * V_HEAD), F32),
        compiler_params=_params(("arbitrary",)),
        name="sample_o_expand",
    )(o_lat, wuv_h)


def _decode_body(pt_ref, qs_ref, ckvn_ref, kpen_ref, cc_hbm, ck_hbm, o_ref,
                 cbuf, kbuf, sem, *, n_pages, cp, page, s_len):
    b = pl.program_id(0)
    nb = pl.num_programs(0)
    nchunk = n_pages // cp
    rows = N_HEADS * s_len

    def copies(bb, j, slot):
        out = []
        for p in range(cp):
            pg = pt_ref[bb * n_pages + j * cp + p]
            out.append(pltpu.make_async_copy(cc_hbm.at[pg], cbuf.at[slot, p], sem.at[0, slot]))
            out.append(pltpu.make_async_copy(ck_hbm.at[pg], kbuf.at[slot, p], sem.at[1, slot]))
        return out

    def start(bb, j, slot):
        for c in copies(bb, j, slot):
            c.start()

    @pl.when(b == 0)
    def _():
        start(0, 0, 0)

    q = qs_ref[...].reshape(rows, KV_LORA + QK_ROPE)
    ql = q[:, :KV_LORA].astype(BF16)
    qp = q[:, KV_LORA:].astype(BF16)

    def accumulate(carry, s, ck):
        m, l, acc = carry
        m_new = jnp.maximum(m, jnp.max(s, axis=-1, keepdims=True))
        corr = jnp.exp(m - m_new)
        p = jnp.exp(s - m_new)
        l = l * corr + jnp.sum(p, axis=-1, keepdims=True)
        acc = acc * corr + _dot(p.astype(BF16), ck)
        return m_new, l, acc

    def body(j, carry):
        slot = lax.rem(b * nchunk + j, 2)

        @pl.when(j + 1 < nchunk)
        def _():
            start(b, j + 1, 1 - slot)

        @pl.when(jnp.logical_and(j + 1 == nchunk, b + 1 < nb))
        def _():
            start(b + 1, 0, 1 - slot)

        for c in copies(b, j, slot):
            c.wait()
        ck = cbuf[slot].reshape(cp * page, KV_LORA).astype(BF16)
        kp = kbuf[slot].reshape(cp * page, QK_ROPE).astype(BF16)
        s = (_dot_nt(ql, ck) + _dot_nt(qp, kp)) * ATTN_SCALE
        return accumulate(carry, s, ck)

    init = (jnp.full((rows, 1), NEG, F32), jnp.zeros((rows, 1), F32),
            jnp.zeros((rows, KV_LORA), F32))
    carry = lax.fori_loop(0, nchunk, body, init)

    pad = 16 - s_len
    cn = jnp.concatenate([ckvn_ref[...], jnp.zeros((pad, KV_LORA), F32)], axis=0).astype(BF16)
    kn = jnp.concatenate([kpen_ref[...], jnp.zeros((pad, QK_ROPE), F32)], axis=0).astype(BF16)
    s = (_dot_nt(ql, cn) + _dot_nt(qp, kn)) * ATTN_SCALE
    r = lax.broadcasted_iota(jnp.int32, s.shape, 0)
    c = lax.broadcasted_iota(jnp.int32, s.shape, 1)
    s = jnp.where(c <= r % s_len, s, NEG)
    _, l, acc = accumulate(carry, s, cn)
    o_ref[...] = (acc / l).reshape(N_HEADS, s_len, KV_LORA)


def _sample_attention(page_table, qs, ckv, kpe, cache_ckv, cache_kpe, n_prompt, s_len):
    dec_batch, n_pages = page_table.shape
    page = cache_ckv.shape[1]
    cp = 8 if n_pages % 8 == 0 else 1
    first = n_prompt // s_len
    grid_spec = pltpu.PrefetchScalarGridSpec(
        num_scalar_prefetch=1,
        grid=(dec_batch,),
        in_specs=[pl.BlockSpec((N_HEADS, s_len, KV_LORA + QK_ROPE), lambda b, pt: (0, b, 0)),
                  pl.BlockSpec((s_len, KV_LORA), lambda b, pt: (first + b, 0)),
                  pl.BlockSpec((s_len, QK_ROPE), lambda b, pt: (first + b, 0)),
                  pl.BlockSpec(memory_space=pl.ANY),
                  pl.BlockSpec(memory_space=pl.ANY)],
        out_specs=pl.BlockSpec((N_HEADS, s_len, KV_LORA), lambda b, pt: (0, b, 0)),
        scratch_shapes=[pltpu.VMEM((2, cp, page, KV_LORA), F32),
                        pltpu.VMEM((2, cp, page, QK_ROPE), F32),
                        pltpu.SemaphoreType.DMA((2, 2))])
    return pl.pallas_call(
        functools.partial(_decode_body, n_pages=n_pages, cp=cp, page=page, s_len=s_len),
        grid_spec=grid_spec,
        out_shape=jax.ShapeDtypeStruct((N_HEADS, dec_batch * s_len, KV_LORA), F32),
        compiler_params=_params(("arbitrary",)),
        name="sample_attention",
    )(page_table.reshape(-1), qs, ckv, kpe, cache_ckv, cache_kpe)


def _merge(c, zc, za, at, x, lg, lb, wpw, bpw, wo):
    mu = jnp.mean(c, axis=-1, keepdims=True)
    cc = c - mu
    var = jnp.mean(cc * cc, axis=-1, keepdims=True)
    y = cc * lax.rsqrt(var + EPS) * lg + lb
    a = (y * jax.nn.sigmoid(y)).astype(BF16)
    conv_out = _dot(a, wpw) + bpw
    merged = (zc * conv_out + za * at).astype(BF16)
    return x + _dot(merged, wo)


def _convmerge_prompt_body(vc_ref, vp_ref, zc_ref, za_ref, at_ref, x_ref, wdw_ref, bdw_ref,
                           lg_ref, lb_ref, wpw_ref, bpw_ref, wo_ref, o_ref, vext, cbuf, *, tc, rb):
    t = pl.program_id(1)
    blk = vext.shape[1] - HALO
    vext[0, 0:HALO, :] = jnp.where(t == 0, 0.0, vp_ref[...])
    for i in range(tc // blk):
        if i:
            vext[i, 0:HALO, :] = vc_ref[i * blk - HALO:i * blk, :]
        vext[i, HALO:HALO + blk, :] = vc_ref[i * blk:(i + 1) * blk, :]
    lead = HALO - (CONV_W - 1)

    def rows(i, _):
        for sb in range(blk // rb):
            acc = jnp.broadcast_to(bdw_ref[...], (rb, D_MODEL))
            for j in range(CONV_W):
                lo = sb * rb + lead + j
                acc = acc + wdw_ref[j:j + 1, :] * vext[i, lo:lo + rb, :]
            cbuf[pl.ds(pl.multiple_of(i * blk + sb * rb, rb), rb), :] = acc
        return 0

    lax.fori_loop(0, tc // blk, rows, 0)
    o_ref[...] = _merge(cbuf[...], zc_ref[...], za_ref[...], at_ref[...], x_ref[...],
                        lg_ref[...], lb_ref[...], wpw_ref[...], bpw_ref[...], wo_ref[...])


def _convmerge_prompt(v, zc, za, attn, x, wdw, bdw, lg, lb, wpw, bpw, wo, batch, seq, tc):
    nt = seq // tc
    per = tc // HALO
    blk = 64 if tc % 64 == 0 else tc
    n_prompt = batch * seq
    cur = pl.BlockSpec((tc, D_MODEL), lambda b, t: (b * nt + t, 0))
    prev = pl.BlockSpec((HALO, D_MODEL), lambda b, t: (jnp.maximum((b * nt + t) * per - 1, 0), 0))
    vec = pl.BlockSpec((1, D_MODEL), lambda b, t: (0, 0))
    mat = pl.BlockSpec((D_MODEL, D_MODEL), lambda b, t: (0, 0))
    return pl.pallas_call(
        functools.partial(_convmerge_prompt_body, tc=tc, rb=16),
        grid=(batch, nt),
        in_specs=[cur, prev, cur, cur, cur, cur,
                  pl.BlockSpec((HALO, D_MODEL), lambda b, t: (0, 0)), vec, vec, vec, mat, vec, mat],
        out_specs=cur,
        out_shape=jax.ShapeDtypeStruct((n_prompt, D_MODEL), F32),
        scratch_shapes=[pltpu.VMEM((tc // blk, HALO + blk, D_MODEL), F32),
                        pltpu.VMEM((tc, D_MODEL), F32)],
        compiler_params=_params(("arbitrary", "arbitrary")),
        name="convmerge_prompt",
    )(v, v, zc, za, attn, x, wdw, bdw, lg, lb, wpw, bpw, wo)


def _convmerge_sample_body(ve_ref, zc_ref, za_ref, at_ref, x_ref, wdw_ref, bdw_ref,
                           lg_ref, lb_ref, wpw_ref, bpw_ref, wo_ref, o_ref, cbuf, *, g, s_len):
    def one(i, _):
        acc = jnp.broadcast_to(bdw_ref[...], (s_len, D_MODEL))
        for j in range(CONV_W):
            acc = acc + wdw_ref[j:j + 1, :] * ve_ref[i, j:j + s_len, :]
        cbuf[pl.ds(pl.multiple_of(i * s_len, s_len), s_len), :] = acc
        return 0

    lax.fori_loop(0, g, one, 0)
    o_ref[...] = _merge(cbuf[...], zc_ref[...], za_ref[...], at_ref[...], x_ref[...],
                        lg_ref[...], lb_ref[...], wpw_ref[...], bpw_ref[...], wo_ref[...])


def _convmerge_sample(vext, zc, za, attn_s, x, wdw, bdw, lg, lb, wpw, bpw, wo, n_prompt, g):
    dec_batch, ext, _ = vext.shape
    s_len = ext - (CONV_W - 1)
    rows = g * s_len
    first = n_prompt // rows
    tok = pl.BlockSpec((rows, D_MODEL), lambda i: (first + i, 0))
    loc = pl.BlockSpec((rows, D_MODEL), lambda i: (i, 0))
    vec = pl.BlockSpec((1, D_MODEL), lambda i: (0, 0))
    mat = pl.BlockSpec((D_MODEL, D_MODEL), lambda i: (0, 0))
    return pl.pallas_call(
        functools.partial(_convmerge_sample_body, g=g, s_len=s_len),
        grid=(dec_batch // g,),
        in_specs=[pl.BlockSpec((g, ext, D_MODEL), lambda i: (i, 0, 0)), tok, tok, loc, tok,
                  pl.BlockSpec((HALO, D_MODEL), lambda i: (0, 0)), vec, vec, vec, mat, vec, mat],
        out_specs=loc,
        out_shape=jax.ShapeDtypeStruct((dec_batch * s_len, D_MODEL), F32),
        scratch_shapes=[pltpu.VMEM((rows, D_MODEL), F32)],
        compiler_params=_params(("arbitrary",)),
        name="convmerge_sample",
    )(vext, zc, za, attn_s, x, wdw, bdw, lg, lb, wpw, bpw, wo)


def _router_body(x_ref, nf_ref, wr_ref, br_ref, h_ref, meta_ref, gate_ref, cnt_ref, carry, *, tm):
    i = pl.program_id(0)

    @pl.when(i == 0)
    def _():
        carry[...] = jnp.zeros_like(carry)

    h = _rms(x_ref[...], nf_ref[...])
    h_ref[...] = h
    hi = h.astype(BF16)
    lo = (h - hi.astype(F32)).astype(BF16)
    r1 = _dot(hi, wr_ref[...])
    r2 = _dot(lo, wr_ref[:, 0:LANES])
    work = r1[:, :LANES] + r1[:, LANES:] + r2 + br_ref[...]
    lane = lax.broadcasted_iota(jnp.int32, (tm, LANES), 1)
    vals, idxs, hots = [], [], []
    for _ in range(TOP_K):
        mk = jnp.max(work, axis=-1, keepdims=True)
        ik = jnp.min(jnp.where(work == mk, lane, LANES), axis=-1, keepdims=True)
        oh = lane == ik
        work = jnp.where(oh, -jnp.inf, work)
        vals.append(mk)
        idxs.append(ik)
        hots.append(oh)
    exps = [jnp.exp(v - vals[0]) for v in vals]
    denom = exps[0] + exps[1] + exps[2] + exps[3]
    chosen = jnp.zeros((tm, LANES), F32)
    for oh in hots:
        chosen = chosen + jnp.where(oh, 1.0, 0.0)
    r = lax.broadcasted_iota(jnp.int32, (tm, tm), 0)
    c = lax.broadcasted_iota(jnp.int32, (tm, tm), 1)
    below = jnp.where(c < r, 1.0, 0.0).astype(BF16)
    rank = _dot(below, chosen.astype(BF16)) + carry[0:1, :]
    meta = jnp.zeros((tm, LANES), jnp.int32)
    gate = jnp.zeros((tm, LANES), F32)
    for k in range(TOP_K):
        rk = jnp.sum(jnp.where(hots[k], rank, 0.0), axis=-1, keepdims=True).astype(jnp.int32)
        meta = jnp.where(lane == k, idxs[k], meta)
        meta = jnp.where(lane == TOP_K + k, rk, meta)
        gate = jnp.where(lane == k, exps[k] / denom, gate)
    meta_ref[...] = meta
    gate_ref[...] = gate
    total = carry[...] + jnp.sum(chosen, axis=0, keepdims=True)
    carry[...] = total
    cnt_ref[...] = total


def _router(x1, nf, wr, br, tm):
    n = x1.shape[0]
    row = lambda w: pl.BlockSpec((tm, w), lambda i: (i, 0))
    return pl.pallas_call(
        functools.partial(_router_body, tm=tm),
        grid=(n // tm,),
        in_specs=[row(D_MODEL), pl.BlockSpec((1, D_MODEL), lambda i: (0, 0)),
                  pl.BlockSpec((D_MODEL, 2 * LANES), lambda i: (0, 0)),
                  pl.BlockSpec((1, LANES), lambda i: (0, 0))],
        out_specs=[row(D_MODEL), row(LANES), row(LANES), pl.BlockSpec((8, LANES), lambda i: (0, 0))],
        out_shape=[jax.ShapeDtypeStruct((n, D_MODEL), F32),
                   jax.ShapeDtypeStruct((n, LANES), jnp.int32),
                   jax.ShapeDtypeStruct((n, LANES), F32),
                   jax.ShapeDtypeStruct((8, LANES), F32)],
        scratch_shapes=[pltpu.VMEM((8, LANES), F32)],
        compiler_params=_params(("arbitrary",)),
        name="router",
    )(x1, nf, wr, br)


def _dispatch_body(slot_ref, h_ref, xs_in, xs_out, sem, *, tm):
    del xs_in
    base = pl.program_id(0) * tm * TOP_K

    def copy(r, k):
        s = slot_ref[base + r * TOP_K + k]
        return pltpu.make_async_copy(h_ref.at[pl.ds(r, 1)], xs_out.at[pl.ds(s, 1)], sem)

    def issue(r, _):
        for k in range(TOP_K):
            copy(r, k).start()
        return 0

    def drain(r, _):
        for k in range(TOP_K):
            copy(r, k).wait()
        return 0

    lax.fori_loop(0, tm, issue, 0)
    lax.fori_loop(0, tm, drain, 0)


def _dispatch(slots, h, xs_zero, tm):
    n = h.shape[0]
    grid_spec = pltpu.PrefetchScalarGridSpec(
        num_scalar_prefetch=1,
        grid=(n // tm,),
        in_specs=[pl.BlockSpec((tm, D_MODEL), lambda i, s: (i, 0)),
                  pl.BlockSpec(memory_space=pl.ANY)],
        out_specs=pl.BlockSpec(memory_space=pl.ANY),
        scratch_shapes=[pltpu.SemaphoreType.DMA(())])
    return pl.pallas_call(
        functools.partial(_dispatch_body, tm=tm),
        grid_spec=grid_spec,
        out_shape=jax.ShapeDtypeStruct(xs_zero.shape, F32),
        input_output_aliases={2: 0},
        compiler_params=_params(("arbitrary",), has_side_effects=True),
        name="dispatch",
    )(slots, h, xs_zero)


def _expert_body(te_ref, nt_ref, xs_ref, w1g_ref, w1l_ref, b1g_ref, b1l_ref, w2_ref, b2_ref, y_ref):
    i = pl.program_id(0)

    @pl.when(i < nt_ref[0])
    def _():
        x = xs_ref[...].astype(BF16)
        glu = jnp.minimum(_dot(x, w1g_ref[0]) + b1g_ref[0], SWIGLU_LIMIT)
        lin = jnp.clip(_dot(x, w1l_ref[0]) + b1l_ref[0], -SWIGLU_LIMIT, SWIGLU_LIMIT)
        act = glu * jax.nn.sigmoid(SWIGLU_ALPHA * glu) * (lin + 1.0)
        y_ref[...] = _dot(act.astype(BF16), w2_ref[0]) + b2_ref[0]

    @pl.when(i >= nt_ref[0])
    def _():
        y_ref[...] = jnp.zeros_like(y_ref)


def _experts(tile_expert, n_tiles, xs, w1g, w1l, b1g, b1l, w2, b2):
    nr = xs.shape[0]
    d_exp = w1g.shape[2]
    tg = EXPERT_TILE
    wspec = lambda a, b: pl.BlockSpec((1, a, b), lambda i, te, nt: (te[i], 0, 0))
    grid_spec = pltpu.PrefetchScalarGridSpec(
        num_scalar_prefetch=2,
        grid=(nr // tg,),
        in_specs=[pl.BlockSpec((tg, D_MODEL), lambda i, te, nt: (i, 0)),
                  wspec(D_MODEL, d_exp), wspec(D_MODEL, d_exp), wspec(1, d_exp), wspec(1, d_exp),
                  wspec(d_exp, D_MODEL), wspec(1, D_MODEL)],
        out_specs=pl.BlockSpec((tg, D_MODEL), lambda i, te, nt: (i, 0)))
    return pl.pallas_call(
        _expert_body,
        grid_spec=grid_spec,
        out_shape=jax.ShapeDtypeStruct((nr, D_MODEL), F32),
        compiler_params=_params(("arbitrary",)),
        name="experts",
    )(tile_expert, n_tiles, xs, w1g, w1l, b1g, b1l, w2, b2)


def _combine_body(slot_ref, x_ref, gate_ref, p_ref, np_ref, wg_ref, wp_ref, nfin_ref, ys_hbm,
                  o_ref, buf, sem, *, tm):
    i = pl.program_id(0)
    n = pl.num_programs(0)

    def copy(ii, sl, r, k):
        s = slot_ref[ii * tm * TOP_K + r * TOP_K + k]
        return pltpu.make_async_copy(ys_hbm.at[pl.ds(s, 1)], buf.at[sl, k, pl.ds(r, 1)], sem.at[sl])

    def issue(ii, sl):
        def row(r, _):
            for k in range(TOP_K):
                copy(ii, sl, r, k).start()
            return 0
        lax.fori_loop(0, tm, row, 0)

    @pl.when(i == 0)
    def _():
        issue(0, 0)

    sl = lax.rem(i, 2)

    @pl.when(i + 1 < n)
    def _():
        issue(i + 1, 1 - sl)

    def drain(r, _):
        for k in range(TOP_K):
            copy(i, sl, r, k).wait()
        return 0

    lax.fori_loop(0, tm, drain, 0)
    g = gate_ref[...]
    x2 = x_ref[...]
    for k in range(TOP_K):
        x2 = x2 + g[:, k:k + 1] * buf[sl, k]
    hn = _rms(x2, np_ref[...]).astype(BF16)
    x3 = x2 + jax.nn.sigmoid(_dot(hn, wg_ref[...])) * _dot(p_ref[...].astype(BF16), wp_ref[...])
    o_ref[...] = _rms(x3, nfin_ref[...])


def _combine(slots, x1, gate, p, npl, wg, wp, nfin, ys, tm):
    n = x1.shape[0]
    ple = p.shape[1]
    row = lambda w: pl.BlockSpec((tm, w), lambda i, s: (i, 0))
    vec = pl.BlockSpec((1, D_MODEL), lambda i, s: (0, 0))
    grid_spec = pltpu.PrefetchScalarGridSpec(
        num_scalar_prefetch=1,
        grid=(n // tm,),
        in_specs=[row(D_MODEL), row(LANES), row(ple), vec,
                  pl.BlockSpec((D_MODEL, D_MODEL), lambda i, s: (0, 0)),
                  pl.BlockSpec((ple, D_MODEL), lambda i, s: (0, 0)), vec,
                  pl.BlockSpec(memory_space=pl.ANY)],
        out_specs=row(D_MODEL),
        scratch_shapes=[pltpu.VMEM((2, TOP_K, tm, D_MODEL), F32), pltpu.SemaphoreType.DMA((2,))])
    return pl.pallas_call(
        functools.partial(_combine_body, tm=tm),
        grid_spec=grid_spec,
        out_shape=jax.ShapeDtypeStruct((n, D_MODEL), F32),
        compiler_params=_params(("arbitrary",)),
        name="combine",
    )(slots, x1, gate, p, npl, wg, wp, nfin, ys)


def _token_tile(n_prompt, n_sample):
    for tm in (256, 128, 64, 32, 16, 8):
        if n_prompt % tm == 0 and n_sample % tm == 0:
            return tm
    raise ValueError("token counts must be multiples of 8")


def _layer(x, cs, p, cache_ckv, cache_kpe, state_conv, page_table, dims,
           norm_mix, w_in, w_dw, b_dw, ln_g, ln_b, w_pw2, b_pw2, norm_q, w_qb, norm_kv, w_kvb, w_o,
           norm_ffn, w_router, b_router, w_e1, b_e1, w_e2, b_e2, norm_ple, w_ple_gate, w_ple_proj,
           norm_out):
    batch, seq, dec_batch, s_len = dims
    n_prompt, n_sample = batch * seq, dec_batch * s_len
    n = n_prompt + n_sample
    tm = _token_tile(n_prompt, n_sample)
    vec = lambda a: a.reshape(1, -1)

    cuts = (2 * D_MODEL, 2 * D_MODEL + Q_LORA, 2 * D_MODEL + Q_LORA + KV_LORA)
    kpe_end = cuts[2] + QK_ROPE
    win = jnp.concatenate([w_in[:, :kpe_end], jnp.zeros((D_MODEL, _C_ZC - _C_KPE - QK_ROPE), F32),
                           w_in[:, kpe_end:]], axis=1).astype(BF16)
    wq3 = w_qb.reshape(Q_LORA, N_HEADS, QK_NOPE + QK_ROPE)
    wqb = jnp.concatenate([wq3[:, :, :QK_NOPE].reshape(Q_LORA, -1),
                           wq3[:, :, QK_NOPE:].reshape(Q_LORA, -1)], axis=1).astype(BF16)
    wkv3 = w_kvb.reshape(KV_LORA, N_HEADS, QK_NOPE + V_HEAD)
    wkv = jnp.concatenate([wkv3[:, :, :QK_NOPE].reshape(KV_LORA, -1),
                           wkv3[:, :, QK_NOPE:].reshape(KV_LORA, -1)], axis=1).astype(BF16)
    wuk_t = jnp.transpose(wkv3[:, :, :QK_NOPE], (1, 2, 0)).astype(BF16)
    wuv_h = jnp.transpose(wkv3[:, :, QK_NOPE:], (1, 0, 2)).astype(BF16)
    wdw = jnp.concatenate([w_dw, jnp.zeros((HALO - CONV_W, D_MODEL), F32)], axis=0)

    v, q_cat, ckv, kpe, zc, za = _inproj(x, cs, vec(norm_mix), win, vec(norm_q), wqb, vec(norm_kv), tm)

    k_cat, v_h = _prompt_kv(ckv, kpe, wkv, n_prompt, tm)
    tq = 512 if seq % 512 == 0 else seq
    attn_p = _prompt_attention(q_cat, k_cat, v_h, batch, seq, tq)

    qs = _sample_q_absorb(q_cat, wuk_t, n_prompt, n_sample)
    o_lat = _sample_attention(page_table, qs, ckv, kpe, cache_ckv, cache_kpe, n_prompt, s_len)
    attn_s = _sample_o_expand(o_lat, wuv_h)

    conv_w = (wdw, vec(b_dw), vec(ln_g), vec(ln_b), w_pw2.astype(BF16), vec(b_pw2), w_o.astype(BF16))
    tc = 256 if seq % 256 == 0 else seq
    x1_p = _convmerge_prompt(v, zc, za, attn_p, x, *conv_w, batch, seq, tc)
    vext = jnp.concatenate([state_conv, v[n_prompt:].reshape(dec_batch, s_len, D_MODEL)], axis=1)
    g = max(1, min(dec_batch, 256 // s_len))
    x1_s = _convmerge_sample(vext, zc, za, attn_s, x, *conv_w, n_prompt, g)
    x1 = jnp.concatenate([x1_p, x1_s], axis=0)

    wr_hi = w_router.astype(BF16)
    wr_lo = (w_router - wr_hi.astype(F32)).astype(BF16)
    padw = jnp.zeros((D_MODEL, LANES - N_EXPERTS), BF16)
    wr = jnp.concatenate([wr_hi, padw, wr_lo, padw], axis=1)
    br = jnp.concatenate([b_router, jnp.full((LANES - N_EXPERTS,), NEG, F32)]).reshape(1, LANES)
    h, meta, gate, cnt = _router(x1, vec(norm_ffn), wr, br, tm)

    tg = EXPERT_TILE
    counts = cnt[0, :N_EXPERTS].astype(jnp.int32)
    tiles_per = (counts + tg - 1) // tg
    tile_end = jnp.cumsum(tiles_per)
    starts = (tile_end - tiles_per) * tg
    n_tiles_max = (n * TOP_K + N_EXPERTS * (tg - 1)) // tg
    slots = (starts[meta[:, :TOP_K]] + meta[:, TOP_K:2 * TOP_K]).reshape(-1)
    tile_expert = jnp.minimum(
        jnp.searchsorted(tile_end, jnp.arange(n_tiles_max, dtype=jnp.int32), side="right"),
        N_EXPERTS - 1).astype(jnp.int32)
    n_tiles = tile_end[-1:].astype(jnp.int32)

    xs = _dispatch(slots, h, jnp.zeros((n_tiles_max * tg, D_MODEL), F32), tm)
    w1g = w_e1[:, :, 0::2].astype(BF16)
    w1l = w_e1[:, :, 1::2].astype(BF16)
    b1g = b_e1[:, None, 0::2]
    b1l = b_e1[:, None, 1::2]
    ys = _experts(tile_expert, n_tiles, xs, w1g, w1l, b1g, b1l, w_e2.astype(BF16), b_e2[:, None, :])

    y = _combine(slots, x1, gate, p, vec(norm_ple), w_ple_gate.astype(BF16),
                 w_ple_proj.astype(BF16), vec(norm_out), ys, tm)
    conv_p = v[:n_prompt].reshape(batch, seq, D_MODEL)[:, seq - (CONV_W - 1):]
    conv_s = vext[:, s_len:]
    return y, ckv, kpe, conv_p, conv_s


def kernel(x_prompt, x_sample, cache_ckv, cache_kpe, state_conv, page_table, p_prompt, p_sample, norm_mix, w_in, w_dw, b_dw, ln_conv_g, ln_conv_b, w_pw2, b_pw2, norm_q, w_qb, norm_kv, w_kvb, w_o, norm_ffn, w_router, b_router, w_e1, b_e1, w_e2, b_e2, norm_ple, w_ple_gate, w_ple_proj, norm_final):
    depth = w_in.shape[0]
    batch, seq, _ = x_prompt.shape
    dec_batch, s_len, _ = x_sample.shape
    n_prompt, n_sample = batch * seq, dec_batch * s_len
    assert n_prompt % n_sample == 0 and seq >= HALO and s_len <= 16
    past_len = page_table.shape[1] * cache_ckv.shape[2]

    half = QK_ROPE // 2
    inv_freq = ROPE_THETA ** (-jnp.arange(half, dtype=F32) / half)
    pos = jnp.concatenate([jnp.tile(jnp.arange(seq), batch),
                           jnp.tile(past_len + jnp.arange(s_len), dec_batch)]).astype(F32)
    ang = pos[:, None] * inv_freq[None, :]
    cos, sin = jnp.cos(ang), jnp.sin(ang)
    cs = jnp.concatenate([cos, cos, cos, cos, -sin, sin, -sin, sin], axis=1)

    x = jnp.concatenate([x_prompt.reshape(n_prompt, D_MODEL), x_sample.reshape(n_sample, D_MODEL)], axis=0)
    outs = ([], [], [], [], [], [])
    for i in range(depth):
        p = jnp.concatenate([p_prompt[i].reshape(n_prompt, -1), p_sample[i].reshape(n_sample, -1)], axis=0)
        assert depth == 1, "layer chaining needs an un-normalised layer output"
        y, ckv, kpe, conv_p, conv_s = _layer(
            x, cs, p, cache_ckv[i], cache_kpe[i], state_conv[i], page_table,
            (batch, seq, dec_batch, s_len),
            norm_mix[i], w_in[i], w_dw[i], b_dw[i], ln_conv_g[i], ln_conv_b[i], w_pw2[i], b_pw2[i],
            norm_q[i], w_qb[i], norm_kv[i], w_kvb[i], w_o[i], norm_ffn[i], w_router[i], b_router[i],
            w_e1[i], b_e1[i], w_e2[i], b_e2[i], norm_ple[i], w_ple_gate[i], w_ple_proj[i], norm_final)
        outs[0].append(ckv[:n_prompt].reshape(batch, seq, KV_LORA))
        outs[1].append(kpe[:n_prompt].reshape(batch, seq, QK_ROPE))
        outs[2].append(conv_p)
        outs[3].append(ckv[n_prompt:].reshape(dec_batch, s_len, KV_LORA))
        outs[4].append(kpe[n_prompt:].reshape(dec_batch, s_len, QK_ROPE))
        outs[5].append(conv_s)
    y_prompt = y[:n_prompt].reshape(batch, seq, D_MODEL)
    y_sample = y[n_prompt:].reshape(dec_batch, s_len, D_MODEL)
    return (y_prompt, y_sample) + tuple(jnp.stack(o) for o in outs)
```

```python
import functools

import jax
import jax.numpy as jnp
from jax import lax
from jax.experimental import pallas as pl
from jax.experimental.pallas import tpu as pltpu

F32 = jnp.float32
BF16 = jnp.bfloat16

D_MODEL = 1024
N_HEADS = 8
QK_NOPE = 128
QK_ROPE = 64
V_HEAD = 128
Q_LORA = 512
KV_LORA = 512
N_EXPERTS = 32
TOP_K = 4
CONV_W = 31
ROPE_THETA = 10000.0
ATTN_SCALE = (QK_NOPE + QK_ROPE) ** -0.5
SWIGLU_LIMIT = 7.0
SWIGLU_ALPHA = 1.702
EPS = 1e-6
NEG = -1e30

LANES = 128
HALO = 32
EXPERT_TILE = 256
VMEM_LIMIT = 56 * 1024 * 1024

_C_GLU_A, _C_GLU_G, _C_Q, _C_KV, _C_KPE, _C_ZC, _C_ZA, _C_END = (
    0, 1024, 2048, 2560, 3072, 3200, 4224, 5248)


def _params(sem, **kw):
    return pltpu.CompilerParams(dimension_semantics=sem, vmem_limit_bytes=VMEM_LIMIT, **kw)


def _rms(x, g):
    return x * lax.rsqrt(jnp.mean(x * x, axis=-1, keepdims=True) + EPS) * g


def _dot(a, b):
    return jnp.dot(a, b, preferred_element_type=F32)


def _dot_nt(a, b):
    return lax.dot_general(a, b, (((1,), (1,)), ((), ())), preferred_element_type=F32)


def _rope128(x, cos, sin):
    lane = lax.broadcasted_iota(jnp.int32, x.shape, 1)
    rot = jnp.where((lane % QK_ROPE) < QK_ROPE // 2,
                    pltpu.roll(x, LANES - QK_ROPE // 2, 1), pltpu.roll(x, QK_ROPE // 2, 1))
    return x * cos + rot * sin


def _split_specs(tm, width, npt):
    prompt = pl.BlockSpec((tm, width), lambda i, *_: (jnp.minimum(i, npt - 1), 0))
    sample = pl.BlockSpec((tm, width), lambda i, *_: (jnp.maximum(i - npt, 0), 0))
    return prompt, sample


def _pick(i, npt, p_ref, s_ref):
    return jnp.where(i < npt, p_ref[...], s_ref[...])


def _put(i, npt, p_ref, s_ref, val):
    @pl.when(i < npt)
    def _():
        p_ref[...] = val

    @pl.when(i >= npt)
    def _():
        s_ref[...] = val


def _inproj_body(xp_ref, xs_ref, csp_ref, css_ref, nmix_ref, win_ref, nq_ref, wqb_ref, nkv_ref,
                 vp_ref, vs_ref, q_ref, ckvp_ref, ckvs_ref, kpep_ref, kpes_ref, zc_ref, za_ref,
                 *, npt):
    i = pl.program_id(0)
    h = _rms(_pick(i, npt, xp_ref, xs_ref), nmix_ref[...]).astype(BF16)
    cs = _pick(i, npt, csp_ref, css_ref)
    cos = cs[:, 0:LANES]
    sin = cs[:, LANES:2 * LANES]

    def proj(lo, hi):
        return _dot(h, win_ref[:, lo:hi])

    _put(i, npt, vp_ref, vs_ref, proj(_C_GLU_A, _C_GLU_G) * jax.nn.sigmoid(proj(_C_GLU_G, _C_Q)))
    qn = _rms(proj(_C_Q, _C_KV), nq_ref[...]).astype(BF16)
    _put(i, npt, ckvp_ref, ckvs_ref, _rms(proj(_C_KV, _C_KPE), nkv_ref[...]))
    _put(i, npt, kpep_ref, kpes_ref, _rope128(proj(_C_KPE, _C_ZC), cos, sin)[:, :QK_ROPE])
    zc_ref[...] = jax.nn.sigmoid(proj(_C_ZC, _C_ZA))
    za_ref[...] = jax.nn.sigmoid(proj(_C_ZA, _C_END))
    q = _dot(qn, wqb_ref[...])
    nope_w = N_HEADS * QK_NOPE
    for j in range(N_HEADS // 2):
        pe = _rope128(q[:, nope_w + LANES * j:nope_w + LANES * (j + 1)], cos, sin)
        for hh in (2 * j, 2 * j + 1):
            q_ref[hh, :, 0:QK_NOPE] = q[:, QK_NOPE * hh:QK_NOPE * (hh + 1)].astype(BF16)
            lo = QK_ROPE * (hh % 2)
            q_ref[hh, :, QK_NOPE:QK_NOPE + QK_ROPE] = pe[:, lo:lo + QK_ROPE].astype(BF16)


def _inproj(xp, xs, csp, css, nmix, win, nq, wqb, nkv, tm):
    n_prompt, n_sample = xp.shape[0], xs.shape[0]
    n = n_prompt + n_sample
    npt = n_prompt // tm
    per_seq = csp.shape[0] // tm
    row = lambda w: pl.BlockSpec((tm, w), lambda i: (i, 0))
    full = lambda a, b: pl.BlockSpec((a, b), lambda i: (0, 0))
    split = lambda w: _split_specs(tm, w, npt)
    sds = jax.ShapeDtypeStruct
    return pl.pallas_call(
        functools.partial(_inproj_body, npt=npt),
        grid=(n // tm,),
        in_specs=[*split(D_MODEL),
                  pl.BlockSpec((tm, 2 * LANES), lambda i: (i % per_seq, 0)), full(tm, 2 * LANES),
                  full(1, D_MODEL), full(D_MODEL, win.shape[1]),
                  full(1, Q_LORA), full(Q_LORA, wqb.shape[1]), full(1, KV_LORA)],
        out_specs=[*split(D_MODEL),
                   pl.BlockSpec((N_HEADS, tm, QK_NOPE + QK_ROPE), lambda i: (0, i, 0)),
                   *split(KV_LORA), *split(QK_ROPE), row(D_MODEL), row(D_MODEL)],
        out_shape=[sds((n_prompt, D_MODEL), F32), sds((n_sample, D_MODEL), F32),
                   sds((N_HEADS, n, QK_NOPE + QK_ROPE), BF16),
                   sds((n_prompt, KV_LORA), F32), sds((n_sample, KV_LORA), F32),
                   sds((n_prompt, QK_ROPE), F32), sds((n_sample, QK_ROPE), F32),
                   sds((n, D_MODEL), F32), sds((n, D_MODEL), F32)],
        compiler_params=_params(("arbitrary",)),
        name="inproj",
    )(xp, xs, csp, css, nmix, win, nq, wqb, nkv)


def _kv_body(ckv_ref, kpe_ref, wkv_ref, k_ref, v_ref):
    kv = _dot(ckv_ref[...].astype(BF16), wkv_ref[...])
    kp = kpe_ref[...].astype(BF16)
    for hh in range(N_HEADS):
        k_ref[hh, :, 0:QK_NOPE] = kv[:, QK_NOPE * hh:QK_NOPE * (hh + 1)].astype(BF16)
        k_ref[hh, :, QK_NOPE:QK_NOPE + QK_ROPE] = kp
        lo = N_HEADS * QK_NOPE + V_HEAD * hh
        v_ref[hh] = kv[:, lo:lo + V_HEAD].astype(BF16)


def _prompt_kv(ckv, kpe, wkv, tm):
    n_prompt = ckv.shape[0]
    return pl.pallas_call(
        _kv_body,
        grid=(n_prompt // tm,),
        in_specs=[pl.BlockSpec((tm, KV_LORA), lambda i: (i, 0)),
                  pl.BlockSpec((tm, QK_ROPE), lambda i: (i, 0)),
                  pl.BlockSpec(wkv.shape, lambda i: (0, 0))],
        out_specs=[pl.BlockSpec((N_HEADS, tm, QK_NOPE + QK_ROPE), lambda i: (0, i, 0)),
                   pl.BlockSpec((N_HEADS, tm, V_HEAD), lambda i: (0, i, 0))],
        out_shape=[jax.ShapeDtypeStruct((N_HEADS, n_prompt, QK_NOPE + QK_ROPE), BF16),
                   jax.ShapeDtypeStruct((N_HEADS, n_prompt, V_HEAD), BF16)],
        compiler_params=_params(("arbitrary",)),
        name="prompt_kv",
    )(ckv, kpe, wkv)


def _attn_body(q_ref, k_ref, v_ref, o_ref, *, tq):
    qi = pl.program_id(2)
    q = q_ref[0]
    row = lax.broadcasted_iota(jnp.int32, (tq, tq), 0)
    col = lax.broadcasted_iota(jnp.int32, (tq, tq), 1)

    def step(j, carry, diagonal):
        m, l, acc = carry
        off = pl.multiple_of(j * tq, tq)
        k = k_ref[0, pl.ds(off, tq), :]
        v = v_ref[0, pl.ds(off, tq), :]
        s = _dot_nt(q, k) * ATTN_SCALE
        if diagonal:
            s = jnp.where(col <= row, s, NEG)
        m_new = jnp.maximum(m, jnp.max(s, axis=-1, keepdims=True))
        corr = jnp.exp(m - m_new)
        p = jnp.exp(s - m_new)
        l = l * corr + jnp.sum(p, axis=-1, keepdims=True)
        acc = acc * corr + _dot(p.astype(BF16), v)
        return m_new, l, acc

    init = (jnp.full((tq, 1), NEG, F32), jnp.zeros((tq, 1), F32), jnp.zeros((tq, V_HEAD), F32))
    carry = lax.fori_loop(0, qi, lambda j, c: step(j, c, False), init)
    _, l, acc = step(qi, carry, True)
    o_ref[...] = acc / l


def _prompt_attention(q_cat, k_cat, v_h, batch, seq, tq):
    nq = seq // tq
    n_prompt = batch * seq
    return pl.pallas_call(
        functools.partial(_attn_body, tq=tq),
        grid=(batch, N_HEADS, nq),
        in_specs=[pl.BlockSpec((1, tq, QK_NOPE + QK_ROPE), lambda b, h, i: (h, b * nq + i, 0)),
                  pl.BlockSpec((1, seq, QK_NOPE + QK_ROPE), lambda b, h, i: (h, b, 0)),
                  pl.BlockSpec((1, seq, V_HEAD), lambda b, h, i: (h, b, 0))],
        out_specs=pl.BlockSpec((tq, V_HEAD), lambda b, h, i: (b * nq + i, h)),
        out_shape=jax.ShapeDtypeStruct((n_prompt, N_HEADS * V_HEAD), F32),
        compiler_params=_params(("arbitrary", "arbitrary", "arbitrary")),
        name="prompt_attention",
    )(q_cat, k_cat, v_h)


def _qabs_body(q_ref, wuk_ref, o_ref):
    q = q_ref[0]
    o_ref[0, :, 0:KV_LORA] = _dot(q[:, :QK_NOPE], wuk_ref[0])
    o_ref[0, :, KV_LORA:KV_LORA + QK_ROPE] = q[:, QK_NOPE:].astype(F32)


def _sample_q_absorb(q_cat, wuk_t, n_prompt, n_sample):
    blk = n_prompt // n_sample
    return pl.pallas_call(
        _qabs_body,
        grid=(N_HEADS,),
        in_specs=[pl.BlockSpec((1, n_sample, QK_NOPE + QK_ROPE), lambda h: (h, blk, 0)),
                  pl.BlockSpec((1, QK_NOPE, KV_LORA), lambda h: (h, 0, 0))],
        out_specs=pl.BlockSpec((1, n_sample, KV_LORA + QK_ROPE), lambda h: (h, 0, 0)),
        out_shape=jax.ShapeDtypeStruct((N_HEADS, n_sample, KV_LORA + QK_ROPE), F32),
        compiler_params=_params(("arbitrary",)),
        name="sample_q_absorb",
    )(q_cat, wuk_t)


def _oabs_body(o_ref, wuv_ref, out_ref):
    out_ref[...] = _dot(o_ref[0].astype(BF16), wuv_ref[0])


def _sample_o_expand(o_lat, wuv_h):
    n_sample = o_lat.shape[1]
    return pl.pallas_call(
        _oabs_body,
        grid=(N_HEADS,),
        in_specs=[pl.BlockSpec((1, n_sample, KV_LORA), lambda h: (h, 0, 0)),
                  pl.BlockSpec((1, KV_LORA, V_HEAD), lambda h: (h, 0, 0))],
        out_specs=pl.BlockSpec((n_sample, V_HEAD), lambda h: (0, h)),
        out_shape=jax.ShapeDtypeStruct((n_sample, N_HEADS * V_HEAD), F32),
        compiler_params=_params(("arbitrary",)),
        name="sample_o_expand",
    )(o_lat, wuv_h)


def _decode_body(pt_ref, qs_ref, ckvn_ref, kpen_ref, cc_hbm, ckt_hbm, o_ref,
                 cbuf, kbuf, sem, *, n_pages, cp, page, s_len):
    b = pl.program_id(0)
    nb = pl.num_programs(0)
    nchunk = n_pages // cp
    rows = N_HEADS * s_len

    def copies(bb, j, slot):
        out = []
        for p in range(cp):
            pg = pt_ref[bb * n_pages + j * cp + p]
            out.append(pltpu.make_async_copy(cc_hbm.at[pg], cbuf.at[slot, p], sem.at[0, slot]))
            out.append(pltpu.make_async_copy(ckt_hbm.at[pg], kbuf.at[slot, p], sem.at[1, slot]))
        return out

    def start(bb, j, slot):
        for c in copies(bb, j, slot):
            c.start()

    @pl.when(b == 0)
    def _():
        start(0, 0, 0)

    q = qs_ref[...].reshape(rows, KV_LORA + QK_ROPE)
    ql = q[:, :KV_LORA].astype(BF16)
    qp = q[:, KV_LORA:].astype(BF16)

    def accumulate(carry, s, ck):
        m, l, acc = carry
        m_new = jnp.maximum(m, jnp.max(s, axis=-1, keepdims=True))
        corr = jnp.exp(m - m_new)
        p = jnp.exp(s - m_new)
        l = l * corr + jnp.sum(p, axis=-1, keepdims=True)
        acc = acc * corr + _dot(p.astype(BF16), ck)
        return m_new, l, acc

    def body(j, carry):
        slot = lax.rem(b * nchunk + j, 2)

        @pl.when(j + 1 < nchunk)
        def _():
            start(b, j + 1, 1 - slot)

        @pl.when(jnp.logical_and(j + 1 == nchunk, b + 1 < nb))
        def _():
            start(b + 1, 0, 1 - slot)

        for c in copies(b, j, slot):
            c.wait()
        ck = cbuf[slot].reshape(cp * page, KV_LORA).astype(BF16)
        kt = jnp.concatenate([kbuf[slot, p] for p in range(cp)], axis=1).astype(BF16)
        s = (_dot_nt(ql, ck) + _dot(qp, kt)) * ATTN_SCALE
        return accumulate(carry, s, ck)

    init = (jnp.full((rows, 1), NEG, F32), jnp.zeros((rows, 1), F32),
            jnp.zeros((rows, KV_LORA), F32))
    carry = lax.fori_loop(0, nchunk, body, init)

    pad = 16 - s_len
    cn = jnp.concatenate([ckvn_ref[...], jnp.zeros((pad, KV_LORA), F32)], axis=0).astype(BF16)
    kn = jnp.concatenate([kpen_ref[...], jnp.zeros((pad, QK_ROPE), F32)], axis=0).astype(BF16)
    s = (_dot_nt(ql, cn) + _dot_nt(qp, kn)) * ATTN_SCALE
    r = lax.broadcasted_iota(jnp.int32, s.shape, 0)
    c = lax.broadcasted_iota(jnp.int32, s.shape, 1)
    s = jnp.where(c <= r % s_len, s, NEG)
    _, l, acc = accumulate(carry, s, cn)
    o_ref[...] = (acc / l).reshape(N_HEADS, s_len, KV_LORA)


def _sample_attention(page_table, qs, ckv_s, kpe_s, cache_ckv, cache_kpe_t, s_len):
    dec_batch, n_pages = page_table.shape
    page = cache_ckv.shape[1]
    cp = next(c for c in (32, 8, 1) if n_pages % c == 0)
    grid_spec = pltpu.PrefetchScalarGridSpec(
        num_scalar_prefetch=1,
        grid=(dec_batch,),
        in_specs=[pl.BlockSpec((N_HEADS, s_len, KV_LORA + QK_ROPE), lambda b, pt: (0, b, 0)),
                  pl.BlockSpec((s_len, KV_LORA), lambda b, pt: (b, 0)),
                  pl.BlockSpec((s_len, QK_ROPE), lambda b, pt: (b, 0)),
                  pl.BlockSpec(memory_space=pl.ANY),
                  pl.BlockSpec(memory_space=pl.ANY)],
        out_specs=pl.BlockSpec((N_HEADS, s_len, KV_LORA), lambda b, pt: (0, b, 0)),
        scratch_shapes=[pltpu.VMEM((2, cp, page, KV_LORA), F32),
                        pltpu.VMEM((2, cp, QK_ROPE, page), F32),
                        pltpu.SemaphoreType.DMA((2, 2))])
    return pl.pallas_call(
        functools.partial(_decode_body, n_pages=n_pages, cp=cp, page=page, s_len=s_len),
        grid_spec=grid_spec,
        out_shape=jax.ShapeDtypeStruct((N_HEADS, dec_batch * s_len, KV_LORA), F32),
        compiler_params=_params(("arbitrary",)),
        name="sample_attention",
    )(page_table.reshape(-1), qs, ckv_s, kpe_s, cache_ckv, cache_kpe_t)


def _merge(c, zc, za, at, x, lg, lb, wpw, bpw, wo):
    mu = jnp.mean(c, axis=-1, keepdims=True)
    cc = c - mu
    var = jnp.mean(cc * cc, axis=-1, keepdims=True)
    y = cc * lax.rsqrt(var + EPS) * lg + lb
    a = (y * jax.nn.sigmoid(y)).astype(BF16)
    conv_out = _dot(a, wpw) + bpw
    merged = (zc * conv_out + za * at).astype(BF16)
    return x + _dot(merged, wo)


def _convmerge_prompt_body(vc_ref, vp_ref, zc_ref, za_ref, at_ref, x_ref, wdw_ref, bdw_ref,
                           lg_ref, lb_ref, wpw_ref, bpw_ref, wo_ref, o_ref, vext, cbuf, *, tc, rb):
    t = pl.program_id(1)
    blk = vext.shape[1] - HALO
    vext[0, 0:HALO, :] = jnp.where(t == 0, 0.0, vp_ref[...])
    for i in range(tc // blk):
        if i:
            vext[i, 0:HALO, :] = vc_ref[i * blk - HALO:i * blk, :]
        vext[i, HALO:HALO + blk, :] = vc_ref[i * blk:(i + 1) * blk, :]
    lead = HALO - (CONV_W - 1)

    def rows(i, _):
        for sb in range(blk // rb):
            acc = jnp.broadcast_to(bdw_ref[...], (rb, D_MODEL))
            for j in range(CONV_W):
                lo = sb * rb + lead + j
                acc = acc + wdw_ref[j:j + 1, :] * vext[i, lo:lo + rb, :]
            cbuf[pl.ds(pl.multiple_of(i * blk + sb * rb, rb), rb), :] = acc
        return 0

    lax.fori_loop(0, tc // blk, rows, 0)
    o_ref[...] = _merge(cbuf[...], zc_ref[...], za_ref[...], at_ref[...], x_ref[...],
                        lg_ref[...], lb_ref[...], wpw_ref[...], bpw_ref[...], wo_ref[...])


def _convmerge_prompt(v, zc, za, attn, x, wdw, bdw, lg, lb, wpw, bpw, wo, batch, seq, tc):
    nt = seq // tc
    per = tc // HALO
    blk = 64 if tc % 64 == 0 else tc
    n_prompt = batch * seq
    cur = pl.BlockSpec((tc, D_MODEL), lambda b, t: (b * nt + t, 0))
    prev = pl.BlockSpec((HALO, D_MODEL), lambda b, t: (jnp.maximum((b * nt + t) * per - 1, 0), 0))
    vec = pl.BlockSpec((1, D_MODEL), lambda b, t: (0, 0))
    mat = pl.BlockSpec((D_MODEL, D_MODEL), lambda b, t: (0, 0))
    return pl.pallas_call(
        functools.partial(_convmerge_prompt_body, tc=tc, rb=16),
        grid=(batch, nt),
        in_specs=[cur, prev, cur, cur, cur, cur,
                  pl.BlockSpec((HALO, D_MODEL), lambda b, t: (0, 0)), vec, vec, vec, mat, vec, mat],
        out_specs=cur,
        out_shape=jax.ShapeDtypeStruct((n_prompt, D_MODEL), F32),
        scratch_shapes=[pltpu.VMEM((tc // blk, HALO + blk, D_MODEL), F32),
                        pltpu.VMEM((tc, D_MODEL), F32)],
        compiler_params=_params(("arbitrary", "arbitrary")),
        name="convmerge_prompt",
    )(v, v, zc, za, attn, x, wdw, bdw, lg, lb, wpw, bpw, wo)


def _convmerge_sample_body(ve_ref, zc_ref, za_ref, at_ref, x_ref, wdw_ref, bdw_ref,
                           lg_ref, lb_ref, wpw_ref, bpw_ref, wo_ref, o_ref, cbuf, *, g, s_len):
    def one(i, _):
        acc = jnp.broadcast_to(bdw_ref[...], (s_len, D_MODEL))
        for j in range(CONV_W):
            acc = acc + wdw_ref[j:j + 1, :] * ve_ref[i, j:j + s_len, :]
        cbuf[pl.ds(pl.multiple_of(i * s_len, s_len), s_len), :] = acc
        return 0

    lax.fori_loop(0, g, one, 0)
    o_ref[...] = _merge(cbuf[...], zc_ref[...], za_ref[...], at_ref[...], x_ref[...],
                        lg_ref[...], lb_ref[...], wpw_ref[...], bpw_ref[...], wo_ref[...])


def _convmerge_sample(vext, zc, za, attn_s, x_s, wdw, bdw, lg, lb, wpw, bpw, wo, n_prompt, g):
    dec_batch, ext, _ = vext.shape
    s_len = ext - (CONV_W - 1)
    rows = g * s_len
    first = n_prompt // rows
    tok = pl.BlockSpec((rows, D_MODEL), lambda i: (first + i, 0))
    loc = pl.BlockSpec((rows, D_MODEL), lambda i: (i, 0))
    vec = pl.BlockSpec((1, D_MODEL), lambda i: (0, 0))
    mat = pl.BlockSpec((D_MODEL, D_MODEL), lambda i: (0, 0))
    return pl.pallas_call(
        functools.partial(_convmerge_sample_body, g=g, s_len=s_len),
        grid=(dec_batch // g,),
        in_specs=[pl.BlockSpec((g, ext, D_MODEL), lambda i: (i, 0, 0)), tok, tok, loc, loc,
                  pl.BlockSpec((HALO, D_MODEL), lambda i: (0, 0)), vec, vec, vec, mat, vec, mat],
        out_specs=loc,
        out_shape=jax.ShapeDtypeStruct((dec_batch * s_len, D_MODEL), F32),
        scratch_shapes=[pltpu.VMEM((rows, D_MODEL), F32)],
        compiler_params=_params(("arbitrary",)),
        name="convmerge_sample",
    )(vext, zc, za, attn_s, x_s, wdw, bdw, lg, lb, wpw, bpw, wo)


def _router_body(xp_ref, xs_ref, nf_ref, wr_ref, br_ref, h_ref, meta_ref, gate_ref, cnt_ref, carry,
                 *, tm, npt):
    i = pl.program_id(0)

    @pl.when(i == 0)
    def _():
        carry[...] = jnp.zeros_like(carry)

    h = _rms(_pick(i, npt, xp_ref, xs_ref), nf_ref[...])
    h_ref[...] = h
    hi = h.astype(BF16)
    lo = (h - hi.astype(F32)).astype(BF16)
    r1 = _dot(hi, wr_ref[...])
    r2 = _dot(lo, wr_ref[:, 0:LANES])
    work = r1[:, :LANES] + r1[:, LANES:] + r2 + br_ref[...]
    lane = lax.broadcasted_iota(jnp.int32, (tm, LANES), 1)
    vals, idxs, hots = [], [], []
    for _ in range(TOP_K):
        mk = jnp.max(work, axis=-1, keepdims=True)
        ik = jnp.min(jnp.where(work == mk, lane, LANES), axis=-1, keepdims=True)
        oh = lane == ik
        work = jnp.where(oh, -jnp.inf, work)
        vals.append(mk)
        idxs.append(ik)
        hots.append(oh)
    exps = [jnp.exp(v - vals[0]) for v in vals]
    denom = exps[0] + exps[1] + exps[2] + exps[3]
    chosen = jnp.zeros((tm, LANES), F32)
    for oh in hots:
        chosen = chosen + jnp.where(oh, 1.0, 0.0)
    r = lax.broadcasted_iota(jnp.int32, (tm, tm), 0)
    c = lax.broadcasted_iota(jnp.int32, (tm, tm), 1)
    below = jnp.where(c < r, 1.0, 0.0).astype(BF16)
    rank = _dot(below, chosen.astype(BF16)) + carry[0:1, :]
    meta = jnp.zeros((tm, LANES), jnp.int32)
    gate = jnp.zeros((tm, LANES), F32)
    for k in range(TOP_K):
        rk = jnp.sum(jnp.where(hots[k], rank, 0.0), axis=-1, keepdims=True).astype(jnp.int32)
        meta = jnp.where(lane == k, idxs[k], meta)
        meta = jnp.where(lane == TOP_K + k, rk, meta)
        gate = jnp.where(lane == k, exps[k] / denom, gate)
    meta_ref[...] = meta
    gate_ref[...] = gate
    total = carry[...] + jnp.sum(chosen, axis=0, keepdims=True)
    carry[...] = total
    cnt_ref[...] = total


def _router(x1p, x1s, nf, wr, br, tm):
    n_prompt, n_sample = x1p.shape[0], x1s.shape[0]
    n = n_prompt + n_sample
    npt = n_prompt // tm
    row = lambda w: pl.BlockSpec((tm, w), lambda i: (i, 0))
    return pl.pallas_call(
        functools.partial(_router_body, tm=tm, npt=npt),
        grid=(n // tm,),
        in_specs=[*_split_specs(tm, D_MODEL, npt), pl.BlockSpec((1, D_MODEL), lambda i: (0, 0)),
                  pl.BlockSpec((D_MODEL, 2 * LANES), lambda i: (0, 0)),
                  pl.BlockSpec((1, LANES), lambda i: (0, 0))],
        out_specs=[row(D_MODEL), row(LANES), row(LANES), pl.BlockSpec((8, LANES), lambda i: (0, 0))],
        out_shape=[jax.ShapeDtypeStruct((n, D_MODEL), F32),
                   jax.ShapeDtypeStruct((n, LANES), jnp.int32),
                   jax.ShapeDtypeStruct((n, LANES), F32),
                   jax.ShapeDtypeStruct((8, LANES), F32)],
        scratch_shapes=[pltpu.VMEM((8, LANES), F32)],
        compiler_params=_params(("arbitrary",)),
        name="router",
    )(x1p, x1s, nf, wr, br)


def _dispatch_body(slot_ref, last_ref, used_ref, h_ref, xs_out, zbuf, sem, zsem, *, tm):
    i = pl.program_id(0)
    tg = zbuf.shape[0]

    def zero_copy(e):
        return pltpu.make_async_copy(zbuf, xs_out.at[pl.ds(pl.multiple_of(last_ref[e], tg), tg)], zsem)

    @pl.when(i == 0)
    def _():
        zbuf[...] = jnp.zeros_like(zbuf)
        for e in range(2 * N_EXPERTS):
            @pl.when(used_ref[e] > 0)
            def _():
                zero_copy(e).start()
        for e in range(2 * N_EXPERTS):
            @pl.when(used_ref[e] > 0)
            def _():
                zero_copy(e).wait()

    base = i * tm * TOP_K

    def copy(r, k):
        s = slot_ref[base + r * TOP_K + k]
        return pltpu.make_async_copy(h_ref.at[pl.ds(r, 1)], xs_out.at[pl.ds(s, 1)], sem)

    def issue(r, _):
        for k in range(TOP_K):
            copy(r, k).start(priority=k % 2)
        return 0

    def drain(r, _):
        for k in range(TOP_K):
            copy(r, k).wait()
        return 0

    lax.fori_loop(0, tm, issue, 0, unroll=4)
    lax.fori_loop(0, tm, drain, 0, unroll=4)


def _dispatch(slots, last_rows, used, h, n_rows, tm):
    n = h.shape[0]
    grid_spec = pltpu.PrefetchScalarGridSpec(
        num_scalar_prefetch=3,
        grid=(n // tm,),
        in_specs=[pl.BlockSpec((tm, D_MODEL), lambda i, *_: (i, 0))],
        out_specs=pl.BlockSpec(memory_space=pl.ANY),
        scratch_shapes=[pltpu.VMEM((EXPERT_TILE, D_MODEL), F32),
                        pltpu.SemaphoreType.DMA(()), pltpu.SemaphoreType.DMA(())])
    return pl.pallas_call(
        functools.partial(_dispatch_body, tm=tm),
        grid_spec=grid_spec,
        out_shape=jax.ShapeDtypeStruct((n_rows, D_MODEL), F32),
        compiler_params=_params(("arbitrary",), has_side_effects=True, disable_bounds_checks=True),
        name="dispatch",
    )(slots, last_rows, used, h)


def _expert_body(te_ref, nt_ref, xs_ref, w1_ref, b1_ref, w2_ref, b2_ref, y_ref, w1s, w2s, stage):
    i = pl.program_id(0)
    d_exp = w2_ref.shape[1]
    changed = jnp.logical_or(i == 0, te_ref[i] != te_ref[jnp.maximum(i - 1, 0)])

    @pl.when(i == 0)
    def _():
        stage[...] = jnp.zeros_like(stage)

    @pl.when(jnp.logical_and(changed, i < nt_ref[0]))
    def _():
        w1s[...] = w1_ref[0].astype(BF16)
        rows = stage.shape[1] // 2
        for r in range(d_exp // rows):
            for cb in range(D_MODEL // LANES):
                cols = slice(LANES * cb, LANES * (cb + 1))
                stage[cb, pl.ds(0, rows, stride=2), :] = w2_ref[0, rows * r:rows * (r + 1), cols]
                w2s[2 * rows * r:2 * rows * (r + 1), cols] = stage[cb].astype(BF16)

    @pl.when(i < nt_ref[0])
    def _():
        x = xs_ref[...].astype(BF16)
        u = _dot(x, w1s[...]) + b1_ref[0]
        even = lax.broadcasted_iota(jnp.int32, (x.shape[0], LANES), 1) % 2 == 0
        parts = []
        for c in range(u.shape[1] // LANES):
            uc = u[:, LANES * c:LANES * (c + 1)]
            glu = jnp.minimum(uc, SWIGLU_LIMIT)
            lin = pltpu.roll(jnp.clip(uc, -SWIGLU_LIMIT, SWIGLU_LIMIT), LANES - 1, 1)
            act = glu * jax.nn.sigmoid(SWIGLU_ALPHA * glu) * (lin + 1.0)
            parts.append(jnp.where(even, act, 0.0).astype(BF16))
        y_ref[...] = _dot(jnp.concatenate(parts, axis=1), w2s[...]) + b2_ref[0]

    @pl.when(i >= nt_ref[0])
    def _():
        y_ref[...] = jnp.zeros_like(y_ref)


def _experts(tile_expert, n_tiles, xs, w1, b1, w2, b2):
    nr = xs.shape[0]
    d_exp = w2.shape[1]
    tg = EXPERT_TILE
    live = lambda i, te, nt: (jnp.minimum(i, nt[0] - 1), 0)
    wspec = lambda a, b: pl.BlockSpec((1, a, b), lambda i, te, nt: (te[i], 0, 0))
    grid_spec = pltpu.PrefetchScalarGridSpec(
        num_scalar_prefetch=2,
        grid=(nr // tg,),
        in_specs=[pl.BlockSpec((tg, D_MODEL), live),
                  wspec(D_MODEL, 2 * d_exp), wspec(1, 2 * d_exp),
                  wspec(d_exp, D_MODEL), wspec(1, D_MODEL)],
        out_specs=pl.BlockSpec((tg, D_MODEL), lambda i, te, nt: (i, 0)),
        scratch_shapes=[pltpu.VMEM((D_MODEL, 2 * d_exp), BF16),
                        pltpu.VMEM((2 * d_exp, D_MODEL), BF16),
                        pltpu.VMEM((D_MODEL // LANES, 512, LANES), F32)])
    return pl.pallas_call(
        _expert_body,
        grid_spec=grid_spec,
        out_shape=jax.ShapeDtypeStruct((nr, D_MODEL), F32),
        compiler_params=_params(("arbitrary",)),
        name="experts",
    )(tile_expert, n_tiles, xs, w1, b1, w2, b2)


def _combine_body(slot_ref, xp_ref, xs_ref, gate_ref, pp_ref, ps_ref, np_ref, wg_ref, wp_ref,
                  nfin_ref, ys_hbm, op_ref, os_ref, buf, sem, *, tm, npt):
    i = pl.program_id(0)
    n = pl.num_programs(0)

    def copy(ii, sl, r, k):
        s = slot_ref[ii * tm * TOP_K + r * TOP_K + k]
        return pltpu.make_async_copy(ys_hbm.at[pl.ds(s, 1)], buf.at[sl, k, pl.ds(r, 1)], sem.at[sl])

    def issue(ii, sl):
        def row(r, _):
            for k in range(TOP_K):
                copy(ii, sl, r, k).start(priority=k % 2)
            return 0
        lax.fori_loop(0, tm, row, 0, unroll=4)

    @pl.when(i == 0)
    def _():
        issue(0, 0)

    sl = lax.rem(i, 2)

    @pl.when(i + 1 < n)
    def _():
        issue(i + 1, 1 - sl)

    def drain(r, _):
        for k in range(TOP_K):
            copy(i, sl, r, k).wait()
        return 0

    lax.fori_loop(0, tm, drain, 0, unroll=4)
    g = gate_ref[...]
    x2 = _pick(i, npt, xp_ref, xs_ref)
    for k in range(TOP_K):
        x2 = x2 + g[:, k:k + 1] * buf[sl, k]
    hn = _rms(x2, np_ref[...]).astype(BF16)
    emb = _dot(_pick(i, npt, pp_ref, ps_ref).astype(BF16), wp_ref[...])
    x3 = x2 + jax.nn.sigmoid(_dot(hn, wg_ref[...])) * emb
    _put(i, npt, op_ref, os_ref, _rms(x3, nfin_ref[...]))


def _combine(slots, x1p, x1s, gate, pp, ps, npl, wg, wp, nfin, ys, tm):
    n_prompt, n_sample = x1p.shape[0], x1s.shape[0]
    n = n_prompt + n_sample
    npt = n_prompt // tm
    ple = pp.shape[1]
    vec = pl.BlockSpec((1, D_MODEL), lambda i, s: (0, 0))
    grid_spec = pltpu.PrefetchScalarGridSpec(
        num_scalar_prefetch=1,
        grid=(n // tm,),
        in_specs=[*_split_specs(tm, D_MODEL, npt), pl.BlockSpec((tm, LANES), lambda i, s: (i, 0)),
                  *_split_specs(tm, ple, npt), vec,
                  pl.BlockSpec((D_MODEL, D_MODEL), lambda i, s: (0, 0)),
                  pl.BlockSpec((ple, D_MODEL), lambda i, s: (0, 0)), vec,
                  pl.BlockSpec(memory_space=pl.ANY)],
        out_specs=list(_split_specs(tm, D_MODEL, npt)),
        scratch_shapes=[pltpu.VMEM((2, TOP_K, tm, D_MODEL), F32), pltpu.SemaphoreType.DMA((2,))])
    return pl.pallas_call(
        functools.partial(_combine_body, tm=tm, npt=npt),
        grid_spec=grid_spec,
        out_shape=[jax.ShapeDtypeStruct((n_prompt, D_MODEL), F32),
                   jax.ShapeDtypeStruct((n_sample, D_MODEL), F32)],
        compiler_params=_params(("arbitrary",), disable_bounds_checks=True),
        name="combine",
    )(slots, x1p, x1s, gate, pp, ps, npl, wg, wp, nfin, ys)


def _token_tile(n_prompt, n_sample):
    for tm in (256, 128, 64, 32, 16, 8):
        if n_prompt % tm == 0 and n_sample % tm == 0:
            return tm
    raise ValueError("token counts must be multiples of 8")


def _layer(xp, xs, csp, css, pp, ps, cache_ckv, cache_kpe_t, state_conv, page_table, dims,
           norm_mix, w_in, w_dw, b_dw, ln_g, ln_b, w_pw2, b_pw2, norm_q, w_qb, norm_kv, w_kvb, w_o,
           norm_ffn, w_router, b_router, w_e1, b_e1, w_e2, b_e2, norm_ple, w_ple_gate, w_ple_proj,
           norm_out):
    batch, seq, dec_batch, s_len = dims
    n_prompt, n_sample = batch * seq, dec_batch * s_len
    n = n_prompt + n_sample
    tm = _token_tile(n_prompt, n_sample)
    vec = lambda a: a.reshape(1, -1)

    kpe_end = 2 * D_MODEL + Q_LORA + KV_LORA + QK_ROPE
    win = jnp.concatenate([w_in[:, :kpe_end], jnp.zeros((D_MODEL, _C_ZC - _C_KPE - QK_ROPE), F32),
                           w_in[:, kpe_end:]], axis=1).astype(BF16)
    wq3 = w_qb.reshape(Q_LORA, N_HEADS, QK_NOPE + QK_ROPE)
    wqb = jnp.concatenate([wq3[:, :, :QK_NOPE].reshape(Q_LORA, -1),
                           wq3[:, :, QK_NOPE:].reshape(Q_LORA, -1)], axis=1).astype(BF16)
    wkv3 = w_kvb.reshape(KV_LORA, N_HEADS, QK_NOPE + V_HEAD)
    wkv = jnp.concatenate([wkv3[:, :, :QK_NOPE].reshape(KV_LORA, -1),
                           wkv3[:, :, QK_NOPE:].reshape(KV_LORA, -1)], axis=1).astype(BF16)
    wuk_t = jnp.transpose(wkv3[:, :, :QK_NOPE], (1, 2, 0)).astype(BF16)
    wuv_h = jnp.transpose(wkv3[:, :, QK_NOPE:], (1, 0, 2)).astype(BF16)
    wdw = jnp.concatenate([w_dw, jnp.zeros((HALO - CONV_W, D_MODEL), F32)], axis=0)

    v_p, v_s, q_cat, ckv_p, ckv_s, kpe_p, kpe_s, zc, za = _inproj(
        xp, xs, csp, css, vec(norm_mix), win, vec(norm_q), wqb, vec(norm_kv), tm)

    k_cat, v_h = _prompt_kv(ckv_p, kpe_p, wkv, tm)
    tq = 512 if seq % 512 == 0 else seq
    attn_p = _prompt_attention(q_cat, k_cat, v_h, batch, seq, tq)

    qs = _sample_q_absorb(q_cat, wuk_t, n_prompt, n_sample)
    o_lat = _sample_attention(page_table, qs, ckv_s, kpe_s, cache_ckv, cache_kpe_t, s_len)
    attn_s = _sample_o_expand(o_lat, wuv_h)

    conv_w = (wdw, vec(b_dw), vec(ln_g), vec(ln_b), w_pw2.astype(BF16), vec(b_pw2), w_o.astype(BF16))
    tc = 256 if seq % 256 == 0 else seq
    x1_p = _convmerge_prompt(v_p, zc, za, attn_p, xp, *conv_w, batch, seq, tc)
    vext = jnp.concatenate([state_conv, v_s.reshape(dec_batch, s_len, D_MODEL)], axis=1)
    g = max(1, min(dec_batch, 256 // s_len))
    x1_s = _convmerge_sample(vext, zc, za, attn_s, xs, *conv_w, n_prompt, g)

    wr_hi = w_router.astype(BF16)
    wr_lo = (w_router - wr_hi.astype(F32)).astype(BF16)
    padw = jnp.zeros((D_MODEL, LANES - N_EXPERTS), BF16)
    wr = jnp.concatenate([wr_hi, padw, wr_lo, padw], axis=1)
    br = jnp.concatenate([b_router, jnp.full((LANES - N_EXPERTS,), NEG, F32)]).reshape(1, LANES)
    h, meta, gate, cnt = _router(x1_p, x1_s, vec(norm_ffn), wr, br, tm)

    tg = EXPERT_TILE
    counts = cnt[0, :N_EXPERTS].astype(jnp.int32)
    tiles_per = (counts + tg - 1) // tg
    tile_end = jnp.cumsum(tiles_per)
    starts = (tile_end - tiles_per) * tg
    n_tiles_max = (n * TOP_K + N_EXPERTS * (tg - 1)) // tg
    slots = (starts[meta[:, :TOP_K]] + meta[:, TOP_K:2 * TOP_K]).reshape(-1)
    tile_ids = jnp.arange(n_tiles_max, dtype=jnp.int32)
    tile_expert = jnp.minimum(jnp.sum((tile_end[None, :] <= tile_ids[:, None]).astype(jnp.int32), axis=1),
                              N_EXPERTS - 1)
    n_tiles = tile_end[-1:].astype(jnp.int32)
    tail = n_tiles[0] + jnp.arange(N_EXPERTS, dtype=jnp.int32)
    clear_rows = jnp.concatenate([jnp.maximum(tile_end - 1, 0), jnp.minimum(tail, n_tiles_max - 1)]) * tg
    clear_used = jnp.concatenate([tiles_per, (tail < n_tiles_max).astype(jnp.int32)])

    xs_rows = _dispatch(slots, clear_rows, clear_used, h, n_tiles_max * tg, tm)
    ys = _experts(tile_expert, n_tiles, xs_rows, w_e1, b_e1[:, None, :], w_e2, b_e2[:, None, :])

    y_p, y_s = _combine(slots, x1_p, x1_s, gate, pp, ps, vec(norm_ple), w_ple_gate.astype(BF16),
                        w_ple_proj.astype(BF16), vec(norm_out), ys, tm)
    conv_p = v_p.reshape(batch, seq, D_MODEL)[:, seq - (CONV_W - 1):]
    conv_s = vext[:, s_len:]
    return y_p, y_s, ckv_p, ckv_s, kpe_p, kpe_s, conv_p, conv_s


def kernel(x_prompt, x_sample, cache_ckv, cache_kpe, state_conv, page_table, p_prompt, p_sample, norm_mix, w_in, w_dw, b_dw, ln_conv_g, ln_conv_b, w_pw2, b_pw2, norm_q, w_qb, norm_kv, w_kvb, w_o, norm_ffn, w_router, b_router, w_e1, b_e1, w_e2, b_e2, norm_ple, w_ple_gate, w_ple_proj, norm_final):
    depth = w_in.shape[0]
    batch, seq, _ = x_prompt.shape
    dec_batch, s_len, _ = x_sample.shape
    n_prompt, n_sample = batch * seq, dec_batch * s_len
    assert depth == 1 and n_prompt % n_sample == 0 and seq >= HALO and s_len <= 16
    tm = _token_tile(n_prompt, n_sample)
    assert seq % tm == 0 and tm % s_len == 0
    past_len = page_table.shape[1] * cache_ckv.shape[2]

    half = QK_ROPE // 2
    inv_freq = ROPE_THETA ** (-jnp.arange(half, dtype=F32) / half)
    pos = jnp.concatenate([jnp.arange(seq), past_len + jnp.tile(jnp.arange(s_len), tm // s_len)])
    ang = pos.astype(F32)[:, None] * inv_freq[None, :]
    cos, sin = jnp.cos(ang), jnp.sin(ang)
    cs = jnp.concatenate([cos, cos, cos, cos, -sin, sin, -sin, sin], axis=1)

    i = 0
    y_p, y_s, ckv_p, ckv_s, kpe_p, kpe_s, conv_p, conv_s = _layer(
        x_prompt.reshape(n_prompt, D_MODEL), x_sample.reshape(n_sample, D_MODEL), cs[:seq], cs[seq:],
        p_prompt[i].reshape(n_prompt, -1), p_sample[i].reshape(n_sample, -1),
        cache_ckv[i], jnp.swapaxes(cache_kpe[i], 1, 2), state_conv[i], page_table,
        (batch, seq, dec_batch, s_len),
        norm_mix[i], w_in[i], w_dw[i], b_dw[i], ln_conv_g[i], ln_conv_b[i], w_pw2[i], b_pw2[i],
        norm_q[i], w_qb[i], norm_kv[i], w_kvb[i], w_o[i], norm_ffn[i], w_router[i], b_router[i],
        w_e1[i], b_e1[i], w_e2[i], b_e2[i], norm_ple[i], w_ple_gate[i], w_ple_proj[i], norm_final)
    return (y_p.reshape(batch, seq, D_MODEL), y_s.reshape(dec_batch, s_len, D_MODEL),
            ckv_p.reshape(1, batch, seq, KV_LORA), kpe_p.reshape(1, batch, seq, QK_ROPE), conv_p[None],
            ckv_s.reshape(1, dec_batch, s_len, KV_LORA), kpe_s.reshape(1, dec_batch, s_len, QK_ROPE),
            conv_s[None])
```

```python
import functools

import jax
import jax.numpy as jnp
from jax import lax
from jax.experimental import pallas as pl
from jax.experimental.pallas import tpu as pltpu

F32 = jnp.float32
BF16 = jnp.bfloat16

D_MODEL = 1024
N_HEADS = 8
QK_NOPE = 128
QK_ROPE = 64
V_HEAD = 128
Q_LORA = 512
KV_LORA = 512
N_EXPERTS = 32
TOP_K = 4
CONV_W = 31
ROPE_THETA = 10000.0
ATTN_SCALE = (QK_NOPE + QK_ROPE) ** -0.5
SWIGLU_LIMIT = 7.0
SWIGLU_ALPHA = 1.702
EPS = 1e-6
NEG = -1e30

LANES = 128
SUBLANES = 8
HALO = 32
EXPERT_TILE = 256
VMEM_LIMIT = 56 * 1024 * 1024

_C_GLU_A, _C_GLU_G, _C_Q, _C_KV, _C_KPE, _C_ZC, _C_ZA, _C_END = (
    0, 1024, 2048, 2560, 3072, 3200, 4224, 5248)


def _params(sem, **kw):
    return pltpu.CompilerParams(dimension_semantics=sem, vmem_limit_bytes=VMEM_LIMIT, **kw)


def _rms(x, g):
    return x * lax.rsqrt(jnp.mean(x * x, axis=-1, keepdims=True) + EPS) * g


def _dot(a, b):
    return jnp.dot(a, b, preferred_element_type=F32)


def _dot_nt(a, b):
    return lax.dot_general(a, b, (((1,), (1,)), ((), ())), preferred_element_type=F32)


def _rope128(x, cos, sin):
    lane = lax.broadcasted_iota(jnp.int32, x.shape, 1)
    rot = jnp.where((lane % QK_ROPE) < QK_ROPE // 2,
                    pltpu.roll(x, LANES - QK_ROPE // 2, 1), pltpu.roll(x, QK_ROPE // 2, 1))
    return x * cos + rot * sin


def _split_specs(tm, width, npt):
    prompt = pl.BlockSpec((tm, width), lambda i, *_: (jnp.minimum(i, npt - 1), 0))
    sample = pl.BlockSpec((tm, width), lambda i, *_: (jnp.maximum(i - npt, 0), 0))
    return prompt, sample


def _pick(i, npt, p_ref, s_ref):
    return jnp.where(i < npt, p_ref[...], s_ref[...])


def _put(i, npt, p_ref, s_ref, val):
    @pl.when(i < npt)
    def _():
        p_ref[...] = val

    @pl.when(i >= npt)
    def _():
        s_ref[...] = val


def _inproj_body(xp_ref, xs_ref, csp_ref, css_ref, nmix_ref, win_ref, nq_ref, wqb_ref, nkv_ref,
                 vp_ref, vs_ref, q_ref, ckvp_ref, ckvs_ref, kpep_ref, kpes_ref, zc_ref, za_ref,
                 *, npt):
    i = pl.program_id(0)
    h = _rms(_pick(i, npt, xp_ref, xs_ref), nmix_ref[...]).astype(BF16)
    cs = _pick(i, npt, csp_ref, css_ref)
    cos = cs[:, 0:LANES]
    sin = cs[:, LANES:2 * LANES]

    def proj(lo, hi):
        return _dot(h, win_ref[:, lo:hi])

    _put(i, npt, vp_ref, vs_ref, proj(_C_GLU_A, _C_GLU_G) * jax.nn.sigmoid(proj(_C_GLU_G, _C_Q)))
    qn = _rms(proj(_C_Q, _C_KV), nq_ref[...]).astype(BF16)
    _put(i, npt, ckvp_ref, ckvs_ref, _rms(proj(_C_KV, _C_KPE), nkv_ref[...]))
    _put(i, npt, kpep_ref, kpes_ref, _rope128(proj(_C_KPE, _C_ZC), cos, sin)[:, :QK_ROPE])
    zc_ref[...] = jax.nn.sigmoid(proj(_C_ZC, _C_ZA))
    za_ref[...] = jax.nn.sigmoid(proj(_C_ZA, _C_END))
    q = _dot(qn, wqb_ref[...])
    nope_w = N_HEADS * QK_NOPE
    for j in range(N_HEADS // 2):
        pe = _rope128(q[:, nope_w + LANES * j:nope_w + LANES * (j + 1)], cos, sin)
        for hh in (2 * j, 2 * j + 1):
            q_ref[hh, :, 0:QK_NOPE] = q[:, QK_NOPE * hh:QK_NOPE * (hh + 1)].astype(BF16)
            lo = QK_ROPE * (hh % 2)
            q_ref[hh, :, QK_NOPE:QK_NOPE + QK_ROPE] = pe[:, lo:lo + QK_ROPE].astype(BF16)


def _inproj(xp, xs, csp, css, nmix, win, nq, wqb, nkv, tm):
    n_prompt, n_sample = xp.shape[0], xs.shape[0]
    n = n_prompt + n_sample
    npt = n_prompt // tm
    per_seq = csp.shape[0] // tm
    row = lambda w: pl.BlockSpec((tm, w), lambda i: (i, 0))
    full = lambda a, b: pl.BlockSpec((a, b), lambda i: (0, 0))
    split = lambda w: _split_specs(tm, w, npt)
    sds = jax.ShapeDtypeStruct
    return pl.pallas_call(
        functools.partial(_inproj_body, npt=npt),
        grid=(n // tm,),
        in_specs=[*split(D_MODEL),
                  pl.BlockSpec((tm, 2 * LANES), lambda i: (i % per_seq, 0)), full(tm, 2 * LANES),
                  full(1, D_MODEL), full(D_MODEL, win.shape[1]),
                  full(1, Q_LORA), full(Q_LORA, wqb.shape[1]), full(1, KV_LORA)],
        out_specs=[*split(D_MODEL),
                   pl.BlockSpec((N_HEADS, tm, QK_NOPE + QK_ROPE), lambda i: (0, i, 0)),
                   *split(KV_LORA), *split(QK_ROPE), row(D_MODEL), row(D_MODEL)],
        out_shape=[sds((n_prompt, D_MODEL), F32), sds((n_sample, D_MODEL), F32),
                   sds((N_HEADS, n, QK_NOPE + QK_ROPE), BF16),
                   sds((n_prompt, KV_LORA), F32), sds((n_sample, KV_LORA), F32),
                   sds((n_prompt, QK_ROPE), F32), sds((n_sample, QK_ROPE), F32),
                   sds((n, D_MODEL), F32), sds((n, D_MODEL), F32)],
        compiler_params=_params(("arbitrary",)),
        name="inproj",
    )(xp, xs, csp, css, nmix, win, nq, wqb, nkv)


def _kv_body(ckv_ref, kpe_ref, wkv_ref, k_ref, v_ref):
    kv = _dot(ckv_ref[...].astype(BF16), wkv_ref[...])
    kp = kpe_ref[...].astype(BF16)
    for hh in range(N_HEADS):
        k_ref[hh, :, 0:QK_NOPE] = kv[:, QK_NOPE * hh:QK_NOPE * (hh + 1)].astype(BF16)
        k_ref[hh, :, QK_NOPE:QK_NOPE + QK_ROPE] = kp
        lo = N_HEADS * QK_NOPE + V_HEAD * hh
        v_ref[hh] = kv[:, lo:lo + V_HEAD].astype(BF16)


def _prompt_kv(ckv, kpe, wkv, tm):
    n_prompt = ckv.shape[0]
    return pl.pallas_call(
        _kv_body,
        grid=(n_prompt // tm,),
        in_specs=[pl.BlockSpec((tm, KV_LORA), lambda i: (i, 0)),
                  pl.BlockSpec((tm, QK_ROPE), lambda i: (i, 0)),
                  pl.BlockSpec(wkv.shape, lambda i: (0, 0))],
        out_specs=[pl.BlockSpec((N_HEADS, tm, QK_NOPE + QK_ROPE), lambda i: (0, i, 0)),
                   pl.BlockSpec((N_HEADS, tm, V_HEAD), lambda i: (0, i, 0))],
        out_shape=[jax.ShapeDtypeStruct((N_HEADS, n_prompt, QK_NOPE + QK_ROPE), BF16),
                   jax.ShapeDtypeStruct((N_HEADS, n_prompt, V_HEAD), BF16)],
        compiler_params=_params(("arbitrary",)),
        name="prompt_kv",
    )(ckv, kpe, wkv)


def _attn_body(q_ref, k_ref, v_ref, o_ref, *, tq):
    qi = pl.program_id(2)
    q = q_ref[0]
    row = lax.broadcasted_iota(jnp.int32, (tq, tq), 0)
    col = lax.broadcasted_iota(jnp.int32, (tq, tq), 1)

    def step(j, carry, diagonal):
        m, l, acc = carry
        off = pl.multiple_of(j * tq, tq)
        k = k_ref[0, pl.ds(off, tq), :]
        v = v_ref[0, pl.ds(off, tq), :]
        s = _dot_nt(q, k) * ATTN_SCALE
        if diagonal:
            s = jnp.where(col <= row, s, NEG)
        m_new = jnp.maximum(m, jnp.max(s, axis=-1, keepdims=True))
        corr = jnp.exp(m - m_new)
        p = jnp.exp(s - m_new)
        l = l * corr + jnp.sum(p, axis=-1, keepdims=True)
        acc = acc * corr + _dot(p.astype(BF16), v)
        return m_new, l, acc

    init = (jnp.full((tq, 1), NEG, F32), jnp.zeros((tq, 1), F32), jnp.zeros((tq, V_HEAD), F32))
    carry = lax.fori_loop(0, qi, lambda j, c: step(j, c, False), init)
    _, l, acc = step(qi, carry, True)
    o_ref[...] = acc / l


def _prompt_attention(q_cat, k_cat, v_h, batch, seq, tq):
    nq = seq // tq
    n_prompt = batch * seq
    return pl.pallas_call(
        functools.partial(_attn_body, tq=tq),
        grid=(batch, N_HEADS, nq),
        in_specs=[pl.BlockSpec((1, tq, QK_NOPE + QK_ROPE), lambda b, h, i: (h, b * nq + i, 0)),
                  pl.BlockSpec((1, seq, QK_NOPE + QK_ROPE), lambda b, h, i: (h, b, 0)),
                  pl.BlockSpec((1, seq, V_HEAD), lambda b, h, i: (h, b, 0))],
        out_specs=pl.BlockSpec((tq, V_HEAD), lambda b, h, i: (b * nq + i, h)),
        out_shape=jax.ShapeDtypeStruct((n_prompt, N_HEADS * V_HEAD), F32),
        compiler_params=_params(("arbitrary", "arbitrary", "arbitrary")),
        name="prompt_attention",
    )(q_cat, k_cat, v_h)


def _qabs_body(q_ref, wuk_ref, o_ref):
    q = q_ref[0]
    o_ref[0, :, 0:KV_LORA] = _dot(q[:, :QK_NOPE], wuk_ref[0])
    o_ref[0, :, KV_LORA:KV_LORA + QK_ROPE] = q[:, QK_NOPE:].astype(F32)


def _sample_q_absorb(q_cat, wuk_t, n_prompt, n_sample):
    blk = n_prompt // n_sample
    return pl.pallas_call(
        _qabs_body,
        grid=(N_HEADS,),
        in_specs=[pl.BlockSpec((1, n_sample, QK_NOPE + QK_ROPE), lambda h: (h, blk, 0)),
                  pl.BlockSpec((1, QK_NOPE, KV_LORA), lambda h: (h, 0, 0))],
        out_specs=pl.BlockSpec((1, n_sample, KV_LORA + QK_ROPE), lambda h: (h, 0, 0)),
        out_shape=jax.ShapeDtypeStruct((N_HEADS, n_sample, KV_LORA + QK_ROPE), F32),
        compiler_params=_params(("arbitrary",)),
        name="sample_q_absorb",
    )(q_cat, wuk_t)


def _oabs_body(o_ref, wuv_ref, out_ref):
    out_ref[...] = _dot(o_ref[0].astype(BF16), wuv_ref[0])


def _sample_o_expand(o_lat, wuv_h):
    n_sample = o_lat.shape[1]
    return pl.pallas_call(
        _oabs_body,
        grid=(N_HEADS,),
        in_specs=[pl.BlockSpec((1, n_sample, KV_LORA), lambda h: (h, 0, 0)),
                  pl.BlockSpec((1, KV_LORA, V_HEAD), lambda h: (h, 0, 0))],
        out_specs=pl.BlockSpec((n_sample, V_HEAD), lambda h: (0, h)),
        out_shape=jax.ShapeDtypeStruct((n_sample, N_HEADS * V_HEAD), F32),
        compiler_params=_params(("arbitrary",)),
        name="sample_o_expand",
    )(o_lat, wuv_h)


def _decode_body(pt_ref, qs_ref, ckvn_ref, kpen_ref, cc_hbm, ckt_hbm, o_ref,
                 cbuf, kbuf, sem, *, n_pages, cp, page, s_len):
    b = pl.program_id(0)
    nb = pl.num_programs(0)
    nchunk = n_pages // cp
    rows = N_HEADS * s_len

    def copies(bb, j, slot):
        out = []
        for p in range(cp):
            pg = pt_ref[bb * n_pages + j * cp + p]
            out.append(pltpu.make_async_copy(cc_hbm.at[pg], cbuf.at[slot, p], sem.at[0, slot]))
            out.append(pltpu.make_async_copy(ckt_hbm.at[pg], kbuf.at[slot, p], sem.at[1, slot]))
        return out

    def start(bb, j, slot):
        for c in copies(bb, j, slot):
            c.start()

    @pl.when(b == 0)
    def _():
        start(0, 0, 0)

    q = qs_ref[...].reshape(rows, KV_LORA + QK_ROPE)
    ql = q[:, :KV_LORA].astype(BF16)
    qp = q[:, KV_LORA:].astype(BF16)

    def accumulate(carry, s, ck):
        m, l, acc = carry
        m_new = jnp.maximum(m, jnp.max(s, axis=-1, keepdims=True))
        corr = jnp.exp(m - m_new)
        p = jnp.exp(s - m_new)
        l = l * corr + jnp.sum(p, axis=-1, keepdims=True)
        acc = acc * corr + _dot(p.astype(BF16), ck)
        return m_new, l, acc

    def body(j, carry):
        slot = lax.rem(b * nchunk + j, 2)

        @pl.when(j + 1 < nchunk)
        def _():
            start(b, j + 1, 1 - slot)

        @pl.when(jnp.logical_and(j + 1 == nchunk, b + 1 < nb))
        def _():
            start(b + 1, 0, 1 - slot)

        for c in copies(b, j, slot):
            c.wait()
        ck = cbuf[slot].reshape(cp * page, KV_LORA).astype(BF16)
        kt = jnp.concatenate([kbuf[slot, p] for p in range(cp)], axis=1).astype(BF16)
        s = (_dot_nt(ql, ck) + _dot(qp, kt)) * ATTN_SCALE
        return accumulate(carry, s, ck)

    init = (jnp.full((rows, 1), NEG, F32), jnp.zeros((rows, 1), F32),
            jnp.zeros((rows, KV_LORA), F32))
    carry = lax.fori_loop(0, nchunk, body, init)

    pad = 16 - s_len
    cn = jnp.concatenate([ckvn_ref[...], jnp.zeros((pad, KV_LORA), F32)], axis=0).astype(BF16)
    kn = jnp.concatenate([kpen_ref[...], jnp.zeros((pad, QK_ROPE), F32)], axis=0).astype(BF16)
    s = (_dot_nt(ql, cn) + _dot_nt(qp, kn)) * ATTN_SCALE
    r = lax.broadcasted_iota(jnp.int32, s.shape, 0)
    c = lax.broadcasted_iota(jnp.int32, s.shape, 1)
    s = jnp.where(c <= r % s_len, s, NEG)
    _, l, acc = accumulate(carry, s, cn)
    o_ref[...] = (acc / l).reshape(N_HEADS, s_len, KV_LORA)


def _sample_attention(page_table, qs, ckv_s, kpe_s, cache_ckv, cache_kpe_t, s_len):
    dec_batch, n_pages = page_table.shape
    page = cache_ckv.shape[1]
    cp = next(c for c in (32, 8, 1) if n_pages % c == 0)
    grid_spec = pltpu.PrefetchScalarGridSpec(
        num_scalar_prefetch=1,
        grid=(dec_batch,),
        in_specs=[pl.BlockSpec((N_HEADS, s_len, KV_LORA + QK_ROPE), lambda b, pt: (0, b, 0)),
                  pl.BlockSpec((s_len, KV_LORA), lambda b, pt: (b, 0)),
                  pl.BlockSpec((s_len, QK_ROPE), lambda b, pt: (b, 0)),
                  pl.BlockSpec(memory_space=pl.ANY),
                  pl.BlockSpec(memory_space=pl.ANY)],
        out_specs=pl.BlockSpec((N_HEADS, s_len, KV_LORA), lambda b, pt: (0, b, 0)),
        scratch_shapes=[pltpu.VMEM((2, cp, page, KV_LORA), F32),
                        pltpu.VMEM((2, cp, QK_ROPE, page), F32),
                        pltpu.SemaphoreType.DMA((2, 2))])
    return pl.pallas_call(
        functools.partial(_decode_body, n_pages=n_pages, cp=cp, page=page, s_len=s_len),
        grid_spec=grid_spec,
        out_shape=jax.ShapeDtypeStruct((N_HEADS, dec_batch * s_len, KV_LORA), F32),
        compiler_params=_params(("arbitrary",)),
        name="sample_attention",
    )(page_table.reshape(-1), qs, ckv_s, kpe_s, cache_ckv, cache_kpe_t)


def _merge(c, zc, za, at, x, lg, lb, wpw, bpw, wo):
    mu = jnp.mean(c, axis=-1, keepdims=True)
    cc = c - mu
    var = jnp.mean(cc * cc, axis=-1, keepdims=True)
    y = cc * lax.rsqrt(var + EPS) * lg + lb
    a = (y * jax.nn.sigmoid(y)).astype(BF16)
    conv_out = _dot(a, wpw) + bpw
    merged = (zc * conv_out + za * at).astype(BF16)
    return x + _dot(merged, wo)


def _convmerge_prompt_body(vc_ref, vp_ref, zc_ref, za_ref, at_ref, x_ref, wdw_ref, bdw_ref,
                           lg_ref, lb_ref, wpw_ref, bpw_ref, wo_ref, o_ref, vext, cbuf, zs, *, tc, rb):
    t = pl.program_id(1)
    blk = vext.shape[1] - HALO
    vext[0, 0:HALO, :] = jnp.where(t == 0, 0.0, vp_ref[...])
    for i in range(tc // blk):
        if i:
            vext[i, 0:HALO, :] = vc_ref[i * blk - HALO:i * blk, :]
        vext[i, HALO:HALO + blk, :] = vc_ref[i * blk:(i + 1) * blk, :]
    lead = HALO - (CONV_W - 1)

    def rows(i, _):
        for s in range(rb):
            taps = [(a, rb * a + s - lead) for a in range((lead + CONV_W - 1 - s) // rb + 1)
                    if 0 <= rb * a + s - lead < CONV_W]
            for q in range(0, blk + (rb if s else 0), rb):
                acc = None
                for a, j in taps:
                    term = wdw_ref[j] * vext[i, q + rb * a:q + rb * (a + 1), :]
                    acc = term if acc is None else acc + term
                zs[s, q:q + rb, :] = acc
        for q in range(0, blk, rb):
            acc = jnp.broadcast_to(bdw_ref[...], (rb, D_MODEL))
            for s in range(rb):
                acc = acc + zs[s, q + s:q + s + rb, :]
            cbuf[pl.ds(pl.multiple_of(i * blk + q, rb), rb), :] = acc
        return 0

    lax.fori_loop(0, tc // blk, rows, 0)
    o_ref[...] = _merge(cbuf[...], zc_ref[...], za_ref[...], at_ref[...], x_ref[...],
                        lg_ref[...], lb_ref[...], wpw_ref[...], bpw_ref[...], wo_ref[...])


def _convmerge_prompt(v, zc, za, attn, x, wdw, bdw, lg, lb, wpw, bpw, wo, batch, seq, tc):
    nt = seq // tc
    per = tc // HALO
    blk = 64 if tc % 64 == 0 else tc
    n_prompt = batch * seq
    cur = pl.BlockSpec((tc, D_MODEL), lambda b, t: (b * nt + t, 0))
    prev = pl.BlockSpec((HALO, D_MODEL), lambda b, t: (jnp.maximum((b * nt + t) * per - 1, 0), 0))
    vec = pl.BlockSpec((1, D_MODEL), lambda b, t: (0, 0))
    mat = pl.BlockSpec((D_MODEL, D_MODEL), lambda b, t: (0, 0))
    return pl.pallas_call(
        functools.partial(_convmerge_prompt_body, tc=tc, rb=SUBLANES),
        grid=(batch, nt),
        in_specs=[cur, prev, cur, cur, cur, cur,
                  pl.BlockSpec(wdw.shape, lambda b, t: (0, 0, 0)), vec, vec, vec, mat, vec, mat],
        out_specs=cur,
        out_shape=jax.ShapeDtypeStruct((n_prompt, D_MODEL), F32),
        scratch_shapes=[pltpu.VMEM((tc // blk, HALO + blk, D_MODEL), F32),
                        pltpu.VMEM((tc, D_MODEL), F32),
                        pltpu.VMEM((SUBLANES, blk + SUBLANES, D_MODEL), F32)],
        compiler_params=_params(("arbitrary", "arbitrary")),
        name="convmerge_prompt",
    )(v, v, zc, za, attn, x, wdw, bdw, lg, lb, wpw, bpw, wo)


def _convmerge_sample_body(ve_ref, zc_ref, za_ref, at_ref, x_ref, wdw_ref, bdw_ref,
                           lg_ref, lb_ref, wpw_ref, bpw_ref, wo_ref, o_ref, cbuf, *, g, s_len):
    def one(i, _):
        acc = jnp.broadcast_to(bdw_ref[...], (s_len, D_MODEL))
        for j in range(CONV_W):
            acc = acc + wdw_ref[j, 0:s_len, :] * ve_ref[i, j:j + s_len, :]
        cbuf[pl.ds(pl.multiple_of(i * s_len, s_len), s_len), :] = acc
        return 0

    lax.fori_loop(0, g, one, 0)
    o_ref[...] = _merge(cbuf[...], zc_ref[...], za_ref[...], at_ref[...], x_ref[...],
                        lg_ref[...], lb_ref[...], wpw_ref[...], bpw_ref[...], wo_ref[...])


def _convmerge_sample(vext, zc, za, attn_s, x_s, wdw, bdw, lg, lb, wpw, bpw, wo, n_prompt, g):
    dec_batch, ext, _ = vext.shape
    s_len = ext - (CONV_W - 1)
    rows = g * s_len
    first = n_prompt // rows
    tok = pl.BlockSpec((rows, D_MODEL), lambda i: (first + i, 0))
    loc = pl.BlockSpec((rows, D_MODEL), lambda i: (i, 0))
    vec = pl.BlockSpec((1, D_MODEL), lambda i: (0, 0))
    mat = pl.BlockSpec((D_MODEL, D_MODEL), lambda i: (0, 0))
    return pl.pallas_call(
        functools.partial(_convmerge_sample_body, g=g, s_len=s_len),
        grid=(dec_batch // g,),
        in_specs=[pl.BlockSpec((g, ext, D_MODEL), lambda i: (i, 0, 0)), tok, tok, loc, loc,
                  pl.BlockSpec(wdw.shape, lambda i: (0, 0, 0)), vec, vec, vec, mat, vec, mat],
        out_specs=loc,
        out_shape=jax.ShapeDtypeStruct((dec_batch * s_len, D_MODEL), F32),
        scratch_shapes=[pltpu.VMEM((rows, D_MODEL), F32)],
        compiler_params=_params(("arbitrary",)),
        name="convmerge_sample",
    )(vext, zc, za, attn_s, x_s, wdw, bdw, lg, lb, wpw, bpw, wo)


def _router_body(xp_ref, xs_ref, nf_ref, wr_ref, br_ref, h_ref, meta_ref, gate_ref, cnt_ref, carry,
                 *, tm, npt):
    i = pl.program_id(0)

    @pl.when(i == 0)
    def _():
        carry[...] = jnp.zeros_like(carry)

    h = _rms(_pick(i, npt, xp_ref, xs_ref), nf_ref[...])
    h_ref[...] = h
    hi = h.astype(BF16)
    lo = (h - hi.astype(F32)).astype(BF16)
    r1 = _dot(hi, wr_ref[...])
    r2 = _dot(lo, wr_ref[:, 0:LANES])
    work = r1[:, :LANES] + r1[:, LANES:] + r2 + br_ref[...]
    lane = lax.broadcasted_iota(jnp.int32, (tm, LANES), 1)
    vals, idxs, hots = [], [], []
    for _ in range(TOP_K):
        mk = jnp.max(work, axis=-1, keepdims=True)
        ik = jnp.min(jnp.where(work == mk, lane, LANES), axis=-1, keepdims=True)
        oh = lane == ik
        work = jnp.where(oh, -jnp.inf, work)
        vals.append(mk)
        idxs.append(ik)
        hots.append(oh)
    exps = [jnp.exp(v - vals[0]) for v in vals]
    denom = exps[0] + exps[1] + exps[2] + exps[3]
    chosen = jnp.zeros((tm, LANES), F32)
    for oh in hots:
        chosen = chosen + jnp.where(oh, 1.0, 0.0)
    r = lax.broadcasted_iota(jnp.int32, (tm, tm), 0)
    c = lax.broadcasted_iota(jnp.int32, (tm, tm), 1)
    below = jnp.where(c < r, 1.0, 0.0).astype(BF16)
    rank = _dot(below, chosen.astype(BF16)) + carry[0:1, :]
    meta = jnp.zeros((tm, LANES), jnp.int32)
    gate = jnp.zeros((tm, LANES), F32)
    for k in range(TOP_K):
        rk = jnp.sum(jnp.where(hots[k], rank, 0.0), axis=-1, keepdims=True).astype(jnp.int32)
        meta = jnp.where(lane == k, idxs[k], meta)
        meta = jnp.where(lane == TOP_K + k, rk, meta)
        gate = jnp.where(lane == k, exps[k] / denom, gate)
    meta_ref[...] = meta
    gate_ref[...] = gate
    total = carry[...] + jnp.sum(chosen, axis=0, keepdims=True)
    carry[...] = total
    cnt_ref[...] = total


def _router(x1p, x1s, nf, wr, br, tm):
    n_prompt, n_sample = x1p.shape[0], x1s.shape[0]
    n = n_prompt + n_sample
    npt = n_prompt // tm
    row = lambda w: pl.BlockSpec((tm, w), lambda i: (i, 0))
    return pl.pallas_call(
        functools.partial(_router_body, tm=tm, npt=npt),
        grid=(n // tm,),
        in_specs=[*_split_specs(tm, D_MODEL, npt), pl.BlockSpec((1, D_MODEL), lambda i: (0, 0)),
                  pl.BlockSpec((D_MODEL, 2 * LANES), lambda i: (0, 0)),
                  pl.BlockSpec((1, LANES), lambda i: (0, 0))],
        out_specs=[row(D_MODEL), row(LANES), row(LANES), pl.BlockSpec((8, LANES), lambda i: (0, 0))],
        out_shape=[jax.ShapeDtypeStruct((n, D_MODEL), F32),
                   jax.ShapeDtypeStruct((n, LANES), jnp.int32),
                   jax.ShapeDtypeStruct((n, LANES), F32),
                   jax.ShapeDtypeStruct((8, LANES), F32)],
        scratch_shapes=[pltpu.VMEM((8, LANES), F32)],
        compiler_params=_params(("arbitrary",)),
        name="router",
    )(x1p, x1s, nf, wr, br)


def _dispatch_body(slot_ref, last_ref, used_ref, h_ref, xs_out, zbuf, sem, zsem, *, tm):
    i = pl.program_id(0)
    tg = zbuf.shape[0]

    def zero_copy(e):
        return pltpu.make_async_copy(zbuf, xs_out.at[pl.ds(pl.multiple_of(last_ref[e], tg), tg)], zsem)

    @pl.when(i == 0)
    def _():
        zbuf[...] = jnp.zeros_like(zbuf)
        for e in range(2 * N_EXPERTS):
            @pl.when(used_ref[e] > 0)
            def _():
                zero_copy(e).start()
        for e in range(2 * N_EXPERTS):
            @pl.when(used_ref[e] > 0)
            def _():
                zero_copy(e).wait()

    base = i * tm * TOP_K

    def copy(r, k):
        s = slot_ref[base + r * TOP_K + k]
        return pltpu.make_async_copy(h_ref.at[pl.ds(r, 1)], xs_out.at[pl.ds(s, 1)], sem)

    for r in range(tm):
        for k in range(TOP_K):
            copy(r, k).start(priority=k % 2)
    for r in range(tm):
        for k in range(TOP_K):
            copy(r, k).wait()


def _dispatch(slots, last_rows, used, h, n_rows, tm):
    n = h.shape[0]
    grid_spec = pltpu.PrefetchScalarGridSpec(
        num_scalar_prefetch=3,
        grid=(n // tm,),
        in_specs=[pl.BlockSpec((tm, D_MODEL), lambda i, *_: (i, 0))],
        out_specs=pl.BlockSpec(memory_space=pl.ANY),
        scratch_shapes=[pltpu.VMEM((EXPERT_TILE, D_MODEL), F32),
                        pltpu.SemaphoreType.DMA(()), pltpu.SemaphoreType.DMA(())])
    return pl.pallas_call(
        functools.partial(_dispatch_body, tm=tm),
        grid_spec=grid_spec,
        out_shape=jax.ShapeDtypeStruct((n_rows, D_MODEL), F32),
        compiler_params=_params(("arbitrary",), has_side_effects=True, disable_bounds_checks=True),
        name="dispatch",
    )(slots, last_rows, used, h)


def _expert_body(te_ref, nt_ref, xs_ref, w1_ref, b1g_ref, b1l_ref, w2_ref, b2_ref, y_ref,
                 wt, w1g, w1l, w2s):
    i = pl.program_id(0)
    d_exp = w2_ref.shape[1]
    changed = jnp.logical_or(i == 0, te_ref[i] != te_ref[jnp.maximum(i - 1, 0)])

    @pl.when(jnp.logical_and(changed, i < nt_ref[0]))
    def _():
        chunk = 2 * LANES
        for c in range(2 * d_exp // chunk):
            t = w1_ref[0, :, chunk * c:chunk * (c + 1)].T
            for cb in range(D_MODEL // LANES):
                wt[cb, chunk * c:chunk * (c + 1), :] = t[:, LANES * cb:LANES * (cb + 1)]
        for cb in range(D_MODEL // LANES):
            cols = slice(LANES * cb, LANES * (cb + 1))
            w1g[:, cols] = wt[cb, pl.ds(0, d_exp, stride=2), :].astype(BF16)
            w1l[:, cols] = wt[cb, pl.ds(1, d_exp, stride=2), :].astype(BF16)
        w2s[...] = w2_ref[0].astype(BF16)

    @pl.when(i < nt_ref[0])
    def _():
        x = xs_ref[...].astype(BF16)
        glu = jnp.minimum(_dot_nt(x, w1g[...]) + b1g_ref[0], SWIGLU_LIMIT)
        lin = jnp.clip(_dot_nt(x, w1l[...]) + b1l_ref[0], -SWIGLU_LIMIT, SWIGLU_LIMIT)
        act = glu * jax.nn.sigmoid(SWIGLU_ALPHA * glu) * (lin + 1.0)
        y_ref[...] = _dot(act.astype(BF16), w2s[...]) + b2_ref[0]

    @pl.when(i >= nt_ref[0])
    def _():
        y_ref[...] = jnp.zeros_like(y_ref)


def _experts(tile_expert, n_tiles, xs, w1, b1g, b1l, w2, b2):
    nr = xs.shape[0]
    d_exp = w2.shape[1]
    tg = EXPERT_TILE
    live = lambda i, te, nt: (jnp.minimum(i, nt[0] - 1), 0)
    wspec = lambda a, b: pl.BlockSpec((1, a, b), lambda i, te, nt: (te[i], 0, 0))
    grid_spec = pltpu.PrefetchScalarGridSpec(
        num_scalar_prefetch=2,
        grid=(nr // tg,),
        in_specs=[pl.BlockSpec((tg, D_MODEL), live),
                  wspec(D_MODEL, 2 * d_exp), wspec(1, d_exp), wspec(1, d_exp),
                  wspec(d_exp, D_MODEL), wspec(1, D_MODEL)],
        out_specs=pl.BlockSpec((tg, D_MODEL), lambda i, te, nt: (i, 0)),
        scratch_shapes=[pltpu.VMEM((D_MODEL // LANES, 2 * d_exp, LANES), F32),
                        pltpu.VMEM((d_exp, D_MODEL), BF16),
                        pltpu.VMEM((d_exp, D_MODEL), BF16),
                        pltpu.VMEM((d_exp, D_MODEL), BF16)])
    return pl.pallas_call(
        _expert_body,
        grid_spec=grid_spec,
        out_shape=jax.ShapeDtypeStruct((nr, D_MODEL), F32),
        compiler_params=_params(("arbitrary",)),
        name="experts",
    )(tile_expert, n_tiles, xs, w1, b1g, b1l, w2, b2)


def _combine_body(slot_ref, xp_ref, xs_ref, gate_ref, pp_ref, ps_ref, np_ref, wg_ref, wp_ref,
                  nfin_ref, ys_hbm, op_ref, os_ref, buf, sem, *, tm, npt):
    i = pl.program_id(0)
    n = pl.num_programs(0)

    def copy(ii, sl, r, k):
        s = slot_ref[ii * tm * TOP_K + r * TOP_K + k]
        return pltpu.make_async_copy(ys_hbm.at[pl.ds(s, 1)], buf.at[sl, k, pl.ds(r, 1)], sem.at[sl])

    def issue(ii, sl):
        for r in range(tm):
            for k in range(TOP_K):
                copy(ii, sl, r, k).start(priority=k % 2)

    @pl.when(i == 0)
    def _():
        issue(0, 0)

    sl = lax.rem(i, 2)
    for parity in (0, 1):
        @pl.when(sl == parity)
        def _():
            @pl.when(i + 1 < n)
            def _():
                issue(i + 1, 1 - parity)

            for r in range(tm):
                for k in range(TOP_K):
                    copy(i, parity, r, k).wait()

    g = gate_ref[...]
    x2 = _pick(i, npt, xp_ref, xs_ref)
    for k in range(TOP_K):
        x2 = x2 + g[:, k:k + 1] * buf[sl, k]
    hn = _rms(x2, np_ref[...]).astype(BF16)
    emb = _dot(_pick(i, npt, pp_ref, ps_ref).astype(BF16), wp_ref[...])
    x3 = x2 + jax.nn.sigmoid(_dot(hn, wg_ref[...])) * emb
    _put(i, npt, op_ref, os_ref, _rms(x3, nfin_ref[...]))


def _combine(slots, x1p, x1s, gate, pp, ps, npl, wg, wp, nfin, ys, tm):
    n_prompt, n_sample = x1p.shape[0], x1s.shape[0]
    n = n_prompt + n_sample
    npt = n_prompt // tm
    ple = pp.shape[1]
    vec = pl.BlockSpec((1, D_MODEL), lambda i, s: (0, 0))
    grid_spec = pltpu.PrefetchScalarGridSpec(
        num_scalar_prefetch=1,
        grid=(n // tm,),
        in_specs=[*_split_specs(tm, D_MODEL, npt), pl.BlockSpec((tm, LANES), lambda i, s: (i, 0)),
                  *_split_specs(tm, ple, npt), vec,
                  pl.BlockSpec((D_MODEL, D_MODEL), lambda i, s: (0, 0)),
                  pl.BlockSpec((ple, D_MODEL), lambda i, s: (0, 0)), vec,
                  pl.BlockSpec(memory_space=pl.ANY)],
        out_specs=list(_split_specs(tm, D_MODEL, npt)),
        scratch_shapes=[pltpu.VMEM((2, TOP_K, tm, D_MODEL), F32), pltpu.SemaphoreType.DMA((2,))])
    return pl.pallas_call(
        functools.partial(_combine_body, tm=tm, npt=npt),
        grid_spec=grid_spec,
        out_shape=[jax.ShapeDtypeStruct((n_prompt, D_MODEL), F32),
                   jax.ShapeDtypeStruct((n_sample, D_MODEL), F32)],
        compiler_params=_params(("arbitrary",), disable_bounds_checks=True),
        name="combine",
    )(slots, x1p, x1s, gate, pp, ps, npl, wg, wp, nfin, ys)


def _token_tile(n_prompt, n_sample):
    for tm in (256, 128, 64, 32, 16, 8):
        if n_prompt % tm == 0 and n_sample % tm == 0:
            return tm
    raise ValueError("token counts must be multiples of 8")


def _layer(xp, xs, csp, css, pp, ps, cache_ckv, cache_kpe_t, state_conv, page_table, dims,
           norm_mix, w_in, w_dw, b_dw, ln_g, ln_b, w_pw2, b_pw2, norm_q, w_qb, norm_kv, w_kvb, w_o,
           norm_ffn, w_router, b_router, w_e1, b_e1, w_e2, b_e2, norm_ple, w_ple_gate, w_ple_proj,
           norm_out):
    batch, seq, dec_batch, s_len = dims
    n_prompt, n_sample = batch * seq, dec_batch * s_len
    n = n_prompt + n_sample
    tm = _token_tile(n_prompt, n_sample)
    vec = lambda a: a.reshape(1, -1)

    kpe_end = 2 * D_MODEL + Q_LORA + KV_LORA + QK_ROPE
    win = jnp.concatenate([w_in[:, :kpe_end], jnp.zeros((D_MODEL, _C_ZC - _C_KPE - QK_ROPE), F32),
                           w_in[:, kpe_end:]], axis=1).astype(BF16)
    wq3 = w_qb.reshape(Q_LORA, N_HEADS, QK_NOPE + QK_ROPE)
    wqb = jnp.concatenate([wq3[:, :, :QK_NOPE].reshape(Q_LORA, -1),
                           wq3[:, :, QK_NOPE:].reshape(Q_LORA, -1)], axis=1).astype(BF16)
    wkv3 = w_kvb.reshape(KV_LORA, N_HEADS, QK_NOPE + V_HEAD)
    wkv = jnp.concatenate([wkv3[:, :, :QK_NOPE].reshape(KV_LORA, -1),
                           wkv3[:, :, QK_NOPE:].reshape(KV_LORA, -1)], axis=1).astype(BF16)
    wuk_t = jnp.transpose(wkv3[:, :, :QK_NOPE], (1, 2, 0)).astype(BF16)
    wuv_h = jnp.transpose(wkv3[:, :, QK_NOPE:], (1, 0, 2)).astype(BF16)
    wdw = jnp.broadcast_to(w_dw[:, None, :], (CONV_W, SUBLANES, D_MODEL))

    v_p, v_s, q_cat, ckv_p, ckv_s, kpe_p, kpe_s, zc, za = _inproj(
        xp, xs, csp, css, vec(norm_mix), win, vec(norm_q), wqb, vec(norm_kv), tm)

    k_cat, v_h = _prompt_kv(ckv_p, kpe_p, wkv, tm)
    tq = 512 if seq % 512 == 0 else seq
    attn_p = _prompt_attention(q_cat, k_cat, v_h, batch, seq, tq)

    qs = _sample_q_absorb(q_cat, wuk_t, n_prompt, n_sample)
    o_lat = _sample_attention(page_table, qs, ckv_s, kpe_s, cache_ckv, cache_kpe_t, s_len)
    attn_s = _sample_o_expand(o_lat, wuv_h)

    conv_w = (wdw, vec(b_dw), vec(ln_g), vec(ln_b), w_pw2.astype(BF16), vec(b_pw2), w_o.astype(BF16))
    tc = 256 if seq % 256 == 0 else seq
    x1_p = _convmerge_prompt(v_p, zc, za, attn_p, xp, *conv_w, batch, seq, tc)
    vext = jnp.concatenate([state_conv, v_s.reshape(dec_batch, s_len, D_MODEL)], axis=1)
    g = max(1, min(dec_batch, 256 // s_len))
    x1_s = _convmerge_sample(vext, zc, za, attn_s, xs, *conv_w, n_prompt, g)

    wr_hi = w_router.astype(BF16)
    wr_lo = (w_router - wr_hi.astype(F32)).astype(BF16)
    padw = jnp.zeros((D_MODEL, LANES - N_EXPERTS), BF16)
    wr = jnp.concatenate([wr_hi, padw, wr_lo, padw], axis=1)
    br = jnp.concatenate([b_router, jnp.full((LANES - N_EXPERTS,), NEG, F32)]).reshape(1, LANES)
    h, meta, gate, cnt = _router(x1_p, x1_s, vec(norm_ffn), wr, br, tm)

    tg = EXPERT_TILE
    counts = cnt[0, :N_EXPERTS].astype(jnp.int32)
    tiles_per = (counts + tg - 1) // tg
    tile_end = jnp.cumsum(tiles_per)
    starts = (tile_end - tiles_per) * tg
    n_tiles_max = (n * TOP_K + N_EXPERTS * (tg - 1)) // tg
    slots = (starts[meta[:, :TOP_K]] + meta[:, TOP_K:2 * TOP_K]).reshape(-1)
    tile_ids = jnp.arange(n_tiles_max, dtype=jnp.int32)
    tile_expert = jnp.minimum(jnp.sum((tile_end[None, :] <= tile_ids[:, None]).astype(jnp.int32), axis=1),
                              N_EXPERTS - 1)
    n_tiles = tile_end[-1:].astype(jnp.int32)
    tail = n_tiles[0] + jnp.arange(N_EXPERTS, dtype=jnp.int32)
    clear_rows = jnp.concatenate([jnp.maximum(tile_end - 1, 0), jnp.minimum(tail, n_tiles_max - 1)]) * tg
    clear_used = jnp.concatenate([tiles_per, (tail < n_tiles_max).astype(jnp.int32)])

    xs_rows = _dispatch(slots, clear_rows, clear_used, h, n_tiles_max * tg, tm)
    ys = _experts(tile_expert, n_tiles, xs_rows, w_e1, b_e1[:, None, 0::2], b_e1[:, None, 1::2],
                  w_e2, b_e2[:, None, :])

    y_p, y_s = _combine(slots, x1_p, x1_s, gate, pp, ps, vec(norm_ple), w_ple_gate.astype(BF16),
                        w_ple_proj.astype(BF16), vec(norm_out), ys, tm)
    conv_p = v_p.reshape(batch, seq, D_MODEL)[:, seq - (CONV_W - 1):]
    conv_s = vext[:, s_len:]
    return y_p, y_s, ckv_p, ckv_s, kpe_p, kpe_s, conv_p, conv_s


def kernel(x_prompt, x_sample, cache_ckv, cache_kpe, state_conv, page_table, p_prompt, p_sample, norm_mix, w_in, w_dw, b_dw, ln_conv_g, ln_conv_b, w_pw2, b_pw2, norm_q, w_qb, norm_kv, w_kvb, w_o, norm_ffn, w_router, b_router, w_e1, b_e1, w_e2, b_e2, norm_ple, w_ple_gate, w_ple_proj, norm_final):
    depth = w_in.shape[0]
    batch, seq, _ = x_prompt.shape
    dec_batch, s_len, _ = x_sample.shape
    n_prompt, n_sample = batch * seq, dec_batch * s_len
    assert depth == 1 and n_prompt % n_sample == 0 and seq >= HALO and s_len <= SUBLANES
    tm = _token_tile(n_prompt, n_sample)
    assert seq % tm == 0 and tm % s_len == 0
    past_len = page_table.shape[1] * cache_ckv.shape[2]

    half = QK_ROPE // 2
    inv_freq = ROPE_THETA ** (-jnp.arange(half, dtype=F32) / half)
    pos = jnp.concatenate([jnp.arange(seq), past_len + jnp.tile(jnp.arange(s_len), tm // s_len)])
    ang = pos.astype(F32)[:, None] * inv_freq[None, :]
    cos, sin = jnp.cos(ang), jnp.sin(ang)
    cs = jnp.concatenate([cos, cos, cos, cos, -sin, sin, -sin, sin], axis=1)

    i = 0
    y_p, y_s, ckv_p, ckv_s, kpe_p, kpe_s, conv_p, conv_s = _layer(
        x_prompt.reshape(n_prompt, D_MODEL), x_sample.reshape(n_sample, D_MODEL), cs[:seq], cs[seq:],
        p_prompt[i].reshape(n_prompt, -1), p_sample[i].reshape(n_sample, -1),
        cache_ckv[i], jnp.swapaxes(cache_kpe[i], 1, 2), state_conv[i], page_table,
        (batch, seq, dec_batch, s_len),
        norm_mix[i], w_in[i], w_dw[i], b_dw[i], ln_conv_g[i], ln_conv_b[i], w_pw2[i], b_pw2[i],
        norm_q[i], w_qb[i], norm_kv[i], w_kvb[i], w_o[i], norm_ffn[i], w_router[i], b_router[i],
        w_e1[i], b_e1[i], w_e2[i], b_e2[i], norm_ple[i], w_ple_gate[i], w_ple_proj[i], norm_final)
    return (y_p.reshape(batch, seq, D_MODEL), y_s.reshape(dec_batch, s_len, D_MODEL),
            ckv_p.reshape(1, batch, seq, KV_LORA), kpe_p.reshape(1, batch, seq, QK_ROPE), conv_p[None],
            ckv_s.reshape(1, dec_batch, s_len, KV_LORA), kpe_s.reshape(1, dec_batch, s_len, QK_ROPE),
            conv_s[None])
```

```python
import functools

import jax
import jax.numpy as jnp
from jax import lax
from jax.experimental import pallas as pl
from jax.experimental.pallas import tpu as pltpu

F32 = jnp.float32
BF16 = jnp.bfloat16

D_MODEL = 1024
N_HEADS = 8
QK_NOPE = 128
QK_ROPE = 64
V_HEAD = 128
Q_LORA = 512
KV_LORA = 512
N_EXPERTS = 32
TOP_K = 4
CONV_W = 31
ROPE_THETA = 10000.0
ATTN_SCALE = (QK_NOPE + QK_ROPE) ** -0.5
SWIGLU_LIMIT = 7.0
SWIGLU_ALPHA = 1.702
EPS = 1e-6
NEG = -1e30

LANES = 128
SUBLANES = 8
HALO = 32
EXPERT_TILE = 256
ATTN_HEADS_PER_STEP = 2
VMEM_LIMIT = 56 * 1024 * 1024

_C_GLU_A, _C_GLU_G, _C_Q, _C_KV, _C_KPE, _C_ZC, _C_ZA, _C_END = (
    0, 1024, 2048, 2560, 3072, 3200, 4224, 5248)


def _params(sem, **kw):
    return pltpu.CompilerParams(dimension_semantics=sem, vmem_limit_bytes=VMEM_LIMIT, **kw)


def _rms(x, g):
    return x * lax.rsqrt(jnp.mean(x * x, axis=-1, keepdims=True) + EPS) * g


def _dot(a, b):
    return jnp.dot(a, b, preferred_element_type=F32)


def _dot_nt(a, b):
    return lax.dot_general(a, b, (((1,), (1,)), ((), ())), preferred_element_type=F32)


def _rope128(x, cos, sin):
    lane = lax.broadcasted_iota(jnp.int32, x.shape, 1)
    rot = jnp.where((lane % QK_ROPE) < QK_ROPE // 2,
                    pltpu.roll(x, LANES - QK_ROPE // 2, 1), pltpu.roll(x, QK_ROPE // 2, 1))
    return x * cos + rot * sin


def _split_specs(tm, width, npt):
    prompt = pl.BlockSpec((tm, width), lambda i, *_: (jnp.minimum(i, npt - 1), 0))
    sample = pl.BlockSpec((tm, width), lambda i, *_: (jnp.maximum(i - npt, 0), 0))
    return prompt, sample


def _pick(i, npt, p_ref, s_ref):
    return jnp.where(i < npt, p_ref[...], s_ref[...])


def _put(i, npt, p_ref, s_ref, val):
    @pl.when(i < npt)
    def _():
        p_ref[...] = val

    @pl.when(i >= npt)
    def _():
        s_ref[...] = val


def _inproj_body(xp_ref, xs_ref, csp_ref, css_ref, nmix_ref, win_ref, nq_ref, wqb_ref, nkv_ref,
                 vp_ref, vs_ref, q_ref, ckvp_ref, ckvs_ref, kpep_ref, kpes_ref, zc_ref, za_ref,
                 *, npt):
    i = pl.program_id(0)
    h = _rms(_pick(i, npt, xp_ref, xs_ref), nmix_ref[...]).astype(BF16)
    cs = _pick(i, npt, csp_ref, css_ref)
    cos = cs[:, 0:LANES]
    sin = cs[:, LANES:2 * LANES]

    def proj(lo, hi):
        return _dot(h, win_ref[:, lo:hi])

    _put(i, npt, vp_ref, vs_ref, proj(_C_GLU_A, _C_GLU_G) * jax.nn.sigmoid(proj(_C_GLU_G, _C_Q)))
    qn = _rms(proj(_C_Q, _C_KV), nq_ref[...]).astype(BF16)
    _put(i, npt, ckvp_ref, ckvs_ref, _rms(proj(_C_KV, _C_KPE), nkv_ref[...]))
    _put(i, npt, kpep_ref, kpes_ref, _rope128(proj(_C_KPE, _C_ZC), cos, sin)[:, :QK_ROPE])
    zc_ref[...] = jax.nn.sigmoid(proj(_C_ZC, _C_ZA))
    za_ref[...] = jax.nn.sigmoid(proj(_C_ZA, _C_END))
    q = _dot(qn, wqb_ref[...])
    nope_w = N_HEADS * QK_NOPE
    for j in range(N_HEADS // 2):
        pe = _rope128(q[:, nope_w + LANES * j:nope_w + LANES * (j + 1)], cos, sin)
        for hh in (2 * j, 2 * j + 1):
            q_ref[hh, :, 0:QK_NOPE] = q[:, QK_NOPE * hh:QK_NOPE * (hh + 1)].astype(BF16)
            lo = QK_ROPE * (hh % 2)
            q_ref[hh, :, QK_NOPE:QK_NOPE + QK_ROPE] = pe[:, lo:lo + QK_ROPE].astype(BF16)


def _inproj(xp, xs, csp, css, nmix, win, nq, wqb, nkv, tm):
    n_prompt, n_sample = xp.shape[0], xs.shape[0]
    n = n_prompt + n_sample
    npt = n_prompt // tm
    per_seq = csp.shape[0] // tm
    row = lambda w: pl.BlockSpec((tm, w), lambda i: (i, 0))
    full = lambda a, b: pl.BlockSpec((a, b), lambda i: (0, 0))
    split = lambda w: _split_specs(tm, w, npt)
    sds = jax.ShapeDtypeStruct
    return pl.pallas_call(
        functools.partial(_inproj_body, npt=npt),
        grid=(n // tm,),
        in_specs=[*split(D_MODEL),
                  pl.BlockSpec((tm, 2 * LANES), lambda i: (i % per_seq, 0)), full(tm, 2 * LANES),
                  full(1, D_MODEL), full(D_MODEL, win.shape[1]),
                  full(1, Q_LORA), full(Q_LORA, wqb.shape[1]), full(1, KV_LORA)],
        out_specs=[*split(D_MODEL),
                   pl.BlockSpec((N_HEADS, tm, QK_NOPE + QK_ROPE), lambda i: (0, i, 0)),
                   *split(KV_LORA), *split(QK_ROPE), row(D_MODEL), row(D_MODEL)],
        out_shape=[sds((n_prompt, D_MODEL), F32), sds((n_sample, D_MODEL), F32),
                   sds((N_HEADS, n, QK_NOPE + QK_ROPE), BF16),
                   sds((n_prompt, KV_LORA), F32), sds((n_sample, KV_LORA), F32),
                   sds((n_prompt, QK_ROPE), F32), sds((n_sample, QK_ROPE), F32),
                   sds((n, D_MODEL), F32), sds((n, D_MODEL), F32)],
        compiler_params=_params(("arbitrary",)),
        name="inproj",
    )(xp, xs, csp, css, nmix, win, nq, wqb, nkv)


def _kv_body(ckv_ref, kpe_ref, wkv_ref, k_ref, v_ref):
    kv = _dot(ckv_ref[...].astype(BF16), wkv_ref[...])
    kp = kpe_ref[...].astype(BF16)
    for hh in range(N_HEADS):
        k_ref[hh, :, 0:QK_NOPE] = kv[:, QK_NOPE * hh:QK_NOPE * (hh + 1)].astype(BF16)
        k_ref[hh, :, QK_NOPE:QK_NOPE + QK_ROPE] = kp
        lo = N_HEADS * QK_NOPE + V_HEAD * hh
        v_ref[hh] = kv[:, lo:lo + V_HEAD].astype(BF16)


def _prompt_kv(ckv, kpe, wkv, tm):
    n_prompt = ckv.shape[0]
    return pl.pallas_call(
        _kv_body,
        grid=(n_prompt // tm,),
        in_specs=[pl.BlockSpec((tm, KV_LORA), lambda i: (i, 0)),
                  pl.BlockSpec((tm, QK_ROPE), lambda i: (i, 0)),
                  pl.BlockSpec(wkv.shape, lambda i: (0, 0))],
        out_specs=[pl.BlockSpec((N_HEADS, tm, QK_NOPE + QK_ROPE), lambda i: (0, i, 0)),
                   pl.BlockSpec((N_HEADS, tm, V_HEAD), lambda i: (0, i, 0))],
        out_shape=[jax.ShapeDtypeStruct((N_HEADS, n_prompt, QK_NOPE + QK_ROPE), BF16),
                   jax.ShapeDtypeStruct((N_HEADS, n_prompt, V_HEAD), BF16)],
        compiler_params=_params(("arbitrary",)),
        name="prompt_kv",
    )(ckv, kpe, wkv)


def _attn_body(q_ref, k_ref, v_ref, o_ref, *, tq):
    qi = pl.program_id(2)
    heads = q_ref.shape[0]
    row = lax.broadcasted_iota(jnp.int32, (tq, tq), 0)
    col = lax.broadcasted_iota(jnp.int32, (tq, tq), 1)

    def step(j, carry, diagonal):
        off = pl.multiple_of(j * tq, tq)
        out = []
        for hh in range(heads):
            m, l, acc = carry[hh]
            k = k_ref[hh, pl.ds(off, tq), :]
            v = v_ref[hh, pl.ds(off, tq), :]
            s = _dot_nt(q_ref[hh], k) * ATTN_SCALE
            if diagonal:
                s = jnp.where(col <= row, s, NEG)
            m_new = jnp.maximum(m, jnp.max(s, axis=-1, keepdims=True))
            corr = jnp.exp(m - m_new)
            p = jnp.exp(s - m_new)
            l = l * corr + jnp.sum(p, axis=-1, keepdims=True)
            acc = acc * corr + _dot(p.astype(BF16), v)
            out.append((m_new, l, acc))
        return tuple(out)

    init = tuple((jnp.full((tq, 1), NEG, F32), jnp.zeros((tq, 1), F32),
                  jnp.zeros((tq, V_HEAD), F32)) for _ in range(heads))
    carry = lax.fori_loop(0, qi, lambda j, c: step(j, c, False), init)
    carry = step(qi, carry, True)
    for hh in range(heads):
        _, l, acc = carry[hh]
        o_ref[:, V_HEAD * hh:V_HEAD * (hh + 1)] = acc / l


def _prompt_attention(q_cat, k_cat, v_h, batch, seq, tq):
    nq = seq // tq
    n_prompt = batch * seq
    hp = ATTN_HEADS_PER_STEP
    return pl.pallas_call(
        functools.partial(_attn_body, tq=tq),
        grid=(batch, N_HEADS // hp, nq),
        in_specs=[pl.BlockSpec((hp, tq, QK_NOPE + QK_ROPE), lambda b, h, i: (h, b * nq + i, 0)),
                  pl.BlockSpec((hp, seq, QK_NOPE + QK_ROPE), lambda b, h, i: (h, b, 0)),
                  pl.BlockSpec((hp, seq, V_HEAD), lambda b, h, i: (h, b, 0))],
        out_specs=pl.BlockSpec((tq, hp * V_HEAD), lambda b, h, i: (b * nq + i, h)),
        out_shape=jax.ShapeDtypeStruct((n_prompt, N_HEADS * V_HEAD), F32),
        compiler_params=_params(("arbitrary", "arbitrary", "arbitrary")),
        name="prompt_attention",
    )(q_cat, k_cat, v_h)


def _qabs_body(q_ref, wuk_ref, o_ref):
    q = q_ref[0]
    o_ref[0, :, 0:KV_LORA] = _dot(q[:, :QK_NOPE], wuk_ref[0])
    o_ref[0, :, KV_LORA:KV_LORA + QK_ROPE] = q[:, QK_NOPE:].astype(F32)


def _sample_q_absorb(q_cat, wuk_t, n_prompt, n_sample):
    blk = n_prompt // n_sample
    return pl.pallas_call(
        _qabs_body,
        grid=(N_HEADS,),
        in_specs=[pl.BlockSpec((1, n_sample, QK_NOPE + QK_ROPE), lambda h: (h, blk, 0)),
                  pl.BlockSpec((1, QK_NOPE, KV_LORA), lambda h: (h, 0, 0))],
        out_specs=pl.BlockSpec((1, n_sample, KV_LORA + QK_ROPE), lambda h: (h, 0, 0)),
        out_shape=jax.ShapeDtypeStruct((N_HEADS, n_sample, KV_LORA + QK_ROPE), F32),
        compiler_params=_params(("arbitrary",)),
        name="sample_q_absorb",
    )(q_cat, wuk_t)


def _oabs_body(o_ref, wuv_ref, out_ref):
    out_ref[...] = _dot(o_ref[0].astype(BF16), wuv_ref[0])


def _sample_o_expand(o_lat, wuv_h):
    n_sample = o_lat.shape[1]
    return pl.pallas_call(
        _oabs_body,
        grid=(N_HEADS,),
        in_specs=[pl.BlockSpec((1, n_sample, KV_LORA), lambda h: (h, 0, 0)),
                  pl.BlockSpec((1, KV_LORA, V_HEAD), lambda h: (h, 0, 0))],
        out_specs=pl.BlockSpec((n_sample, V_HEAD), lambda h: (0, h)),
        out_shape=jax.ShapeDtypeStruct((n_sample, N_HEADS * V_HEAD), F32),
        compiler_params=_params(("arbitrary",)),
        name="sample_o_expand",
    )(o_lat, wuv_h)


def _decode_body(pt_ref, qs_ref, ckvn_ref, kpen_ref, cc_hbm, ckt_hbm, o_ref,
                 cbuf, kbuf, sem, *, n_pages, cp, page, s_len):
    b = pl.program_id(0)
    nb = pl.num_programs(0)
    nchunk = n_pages // cp
    rows = N_HEADS * s_len

    def copies(bb, j, slot):
        out = []
        for p in range(cp):
            pg = pt_ref[bb * n_pages + j * cp + p]
            out.append(pltpu.make_async_copy(cc_hbm.at[pg], cbuf.at[slot, p], sem.at[0, slot]))
            out.append(pltpu.make_async_copy(ckt_hbm.at[pg], kbuf.at[slot, p], sem.at[1, slot]))
        return out

    def start(bb, j, slot):
        for c in copies(bb, j, slot):
            c.start()

    nslot = cbuf.shape[0]
    ahead = nslot - 1

    @pl.when(b == 0)
    def _():
        for t in range(ahead):
            start(0, t, t)

    q = qs_ref[...].reshape(rows, KV_LORA + QK_ROPE)
    ql = q[:, :KV_LORA].astype(BF16)
    qp = q[:, KV_LORA:].astype(BF16)

    def accumulate(carry, s, ck):
        m, l, acc = carry
        m_new = jnp.maximum(m, jnp.max(s, axis=-1, keepdims=True))
        corr = jnp.exp(m - m_new)
        p = jnp.exp(s - m_new)
        l = l * corr + jnp.sum(p, axis=-1, keepdims=True)
        acc = acc * corr + _dot(p.astype(BF16), ck)
        return m_new, l, acc

    def body(j, carry):
        g = b * nchunk + j
        slot = lax.rem(g, nslot)
        nxt = lax.rem(g + ahead, nslot)

        @pl.when(j + ahead < nchunk)
        def _():
            start(b, j + ahead, nxt)

        @pl.when(jnp.logical_and(j + ahead >= nchunk, b + 1 < nb))
        def _():
            start(b + 1, j + ahead - nchunk, nxt)

        for c in copies(b, j, slot):
            c.wait()
        ck = cbuf[slot].reshape(cp * page, KV_LORA).astype(BF16)
        kt = jnp.concatenate([kbuf[slot, p] for p in range(cp)], axis=1).astype(BF16)
        s = (_dot_nt(ql, ck) + _dot(qp, kt)) * ATTN_SCALE
        return accumulate(carry, s, ck)

    init = (jnp.full((rows, 1), NEG, F32), jnp.zeros((rows, 1), F32),
            jnp.zeros((rows, KV_LORA), F32))
    carry = lax.fori_loop(0, nchunk, body, init)

    pad = 16 - s_len
    cn = jnp.concatenate([ckvn_ref[...], jnp.zeros((pad, KV_LORA), F32)], axis=0).astype(BF16)
    kn = jnp.concatenate([kpen_ref[...], jnp.zeros((pad, QK_ROPE), F32)], axis=0).astype(BF16)
    s = (_dot_nt(ql, cn) + _dot_nt(qp, kn)) * ATTN_SCALE
    r = lax.broadcasted_iota(jnp.int32, s.shape, 0)
    c = lax.broadcasted_iota(jnp.int32, s.shape, 1)
    s = jnp.where(c <= r % s_len, s, NEG)
    _, l, acc = accumulate(carry, s, cn)
    o_ref[...] = (acc / l).reshape(N_HEADS, s_len, KV_LORA)


def _sample_attention(page_table, qs, ckv_s, kpe_s, cache_ckv, cache_kpe_t, s_len):
    dec_batch, n_pages = page_table.shape
    page = cache_ckv.shape[1]
    cp = next(c for c in (32, 8, 1) if n_pages % c == 0)
    nslot = min(3, n_pages // cp + 1)
    grid_spec = pltpu.PrefetchScalarGridSpec(
        num_scalar_prefetch=1,
        grid=(dec_batch,),
        in_specs=[pl.BlockSpec((N_HEADS, s_len, KV_LORA + QK_ROPE), lambda b, pt: (0, b, 0)),
                  pl.BlockSpec((s_len, KV_LORA), lambda b, pt: (b, 0)),
                  pl.BlockSpec((s_len, QK_ROPE), lambda b, pt: (b, 0)),
                  pl.BlockSpec(memory_space=pl.ANY),
                  pl.BlockSpec(memory_space=pl.ANY)],
        out_specs=pl.BlockSpec((N_HEADS, s_len, KV_LORA), lambda b, pt: (0, b, 0)),
        scratch_shapes=[pltpu.VMEM((nslot, cp, page, KV_LORA), F32),
                        pltpu.VMEM((nslot, cp, QK_ROPE, page), F32),
                        pltpu.SemaphoreType.DMA((2, nslot))])
    return pl.pallas_call(
        functools.partial(_decode_body, n_pages=n_pages, cp=cp, page=page, s_len=s_len),
        grid_spec=grid_spec,
        out_shape=jax.ShapeDtypeStruct((N_HEADS, dec_batch * s_len, KV_LORA), F32),
        compiler_params=_params(("arbitrary",)),
        name="sample_attention",
    )(page_table.reshape(-1), qs, ckv_s, kpe_s, cache_ckv, cache_kpe_t)


def _merge(c, zc, za, at, x, lg, lb, wpw, bpw, wo):
    mu = jnp.mean(c, axis=-1, keepdims=True)
    cc = c - mu
    var = jnp.mean(cc * cc, axis=-1, keepdims=True)
    y = cc * lax.rsqrt(var + EPS) * lg + lb
    a = (y * jax.nn.sigmoid(y)).astype(BF16)
    conv_out = _dot(a, wpw) + bpw
    merged = (zc * conv_out + za * at).astype(BF16)
    return x + _dot(merged, wo)


def _convmerge_prompt_body(vc_ref, vp_ref, zc_ref, za_ref, at_ref, x_ref, wdw_ref, bdw_ref,
                           lg_ref, lb_ref, wpw_ref, bpw_ref, wo_ref, o_ref, vext, cbuf, zs, *, tc, rb):
    t = pl.program_id(1)
    blk = vext.shape[1] - HALO
    vext[0, 0:HALO, :] = jnp.where(t == 0, 0.0, vp_ref[...])
    for i in range(tc // blk):
        if i:
            vext[i, 0:HALO, :] = vc_ref[i * blk - HALO:i * blk, :]
        vext[i, HALO:HALO + blk, :] = vc_ref[i * blk:(i + 1) * blk, :]
    lead = HALO - (CONV_W - 1)

    def rows(i, _):
        for s in range(rb):
            taps = [(a, rb * a + s - lead) for a in range((lead + CONV_W - 1 - s) // rb + 1)
                    if 0 <= rb * a + s - lead < CONV_W]
            for q in range(0, blk + (rb if s else 0), rb):
                acc = None
                for a, j in taps:
                    term = wdw_ref[j] * vext[i, q + rb * a:q + rb * (a + 1), :]
                    acc = term if acc is None else acc + term
                zs[s, q:q + rb, :] = acc
        for q in range(0, blk, rb):
            acc = jnp.broadcast_to(bdw_ref[...], (rb, D_MODEL))
            for s in range(rb):
                acc = acc + zs[s, q + s:q + s + rb, :]
            cbuf[pl.ds(pl.multiple_of(i * blk + q, rb), rb), :] = acc
        return 0

    lax.fori_loop(0, tc // blk, rows, 0)
    o_ref[...] = _merge(cbuf[...], zc_ref[...], za_ref[...], at_ref[...], x_ref[...],
                        lg_ref[...], lb_ref[...], wpw_ref[...], bpw_ref[...], wo_ref[...])


def _convmerge_prompt(v, zc, za, attn, x, wdw, bdw, lg, lb, wpw, bpw, wo, batch, seq, tc):
    nt = seq // tc
    per = tc // HALO
    blk = 64 if tc % 64 == 0 else tc
    n_prompt = batch * seq
    cur = pl.BlockSpec((tc, D_MODEL), lambda b, t: (b * nt + t, 0))
    prev = pl.BlockSpec((HALO, D_MODEL), lambda b, t: (jnp.maximum((b * nt + t) * per - 1, 0), 0))
    vec = pl.BlockSpec((1, D_MODEL), lambda b, t: (0, 0))
    mat = pl.BlockSpec((D_MODEL, D_MODEL), lambda b, t: (0, 0))
    return pl.pallas_call(
        functools.partial(_convmerge_prompt_body, tc=tc, rb=SUBLANES),
        grid=(batch, nt),
        in_specs=[cur, prev, cur, cur, cur, cur,
                  pl.BlockSpec(wdw.shape, lambda b, t: (0, 0, 0)), vec, vec, vec, mat, vec, mat],
        out_specs=cur,
        out_shape=jax.ShapeDtypeStruct((n_prompt, D_MODEL), F32),
        scratch_shapes=[pltpu.VMEM((tc // blk, HALO + blk, D_MODEL), F32),
                        pltpu.VMEM((tc, D_MODEL), F32),
                        pltpu.VMEM((SUBLANES, blk + SUBLANES, D_MODEL), F32)],
        compiler_params=_params(("arbitrary", "arbitrary")),
        name="convmerge_prompt",
    )(v, v, zc, za, attn, x, wdw, bdw, lg, lb, wpw, bpw, wo)


def _convmerge_sample_body(ve_ref, zc_ref, za_ref, at_ref, x_ref, wdw_ref, bdw_ref,
                           lg_ref, lb_ref, wpw_ref, bpw_ref, wo_ref, o_ref, cbuf, *, g, s_len):
    def one(i, _):
        acc = jnp.broadcast_to(bdw_ref[...], (s_len, D_MODEL))
        for j in range(CONV_W):
            acc = acc + wdw_ref[j, 0:s_len, :] * ve_ref[i, j:j + s_len, :]
        cbuf[pl.ds(pl.multiple_of(i * s_len, s_len), s_len), :] = acc
        return 0

    lax.fori_loop(0, g, one, 0)
    o_ref[...] = _merge(cbuf[...], zc_ref[...], za_ref[...], at_ref[...], x_ref[...],
                        lg_ref[...], lb_ref[...], wpw_ref[...], bpw_ref[...], wo_ref[...])


def _convmerge_sample(vext, zc, za, attn_s, x_s, wdw, bdw, lg, lb, wpw, bpw, wo, n_prompt, g):
    dec_batch, ext, _ = vext.shape
    s_len = ext - (CONV_W - 1)
    rows = g * s_len
    first = n_prompt // rows
    tok = pl.BlockSpec((rows, D_MODEL), lambda i: (first + i, 0))
    loc = pl.BlockSpec((rows, D_MODEL), lambda i: (i, 0))
    vec = pl.BlockSpec((1, D_MODEL), lambda i: (0, 0))
    mat = pl.BlockSpec((D_MODEL, D_MODEL), lambda i: (0, 0))
    return pl.pallas_call(
        functools.partial(_convmerge_sample_body, g=g, s_len=s_len),
        grid=(dec_batch // g,),
        in_specs=[pl.BlockSpec((g, ext, D_MODEL), lambda i: (i, 0, 0)), tok, tok, loc, loc,
                  pl.BlockSpec(wdw.shape, lambda i: (0, 0, 0)), vec, vec, vec, mat, vec, mat],
        out_specs=loc,
        out_shape=jax.ShapeDtypeStruct((dec_batch * s_len, D_MODEL), F32),
        scratch_shapes=[pltpu.VMEM((rows, D_MODEL), F32)],
        compiler_params=_params(("arbitrary",)),
        name="convmerge_sample",
    )(vext, zc, za, attn_s, x_s, wdw, bdw, lg, lb, wpw, bpw, wo)


def _router_body(xp_ref, xs_ref, nf_ref, wr_ref, br_ref, h_ref, meta_ref, gate_ref, cnt_ref, carry,
                 *, tm, npt):
    i = pl.program_id(0)

    @pl.when(i == 0)
    def _():
        carry[...] = jnp.zeros_like(carry)

    h = _rms(_pick(i, npt, xp_ref, xs_ref), nf_ref[...])
    h_ref[...] = h
    hi = h.astype(BF16)
    lo = (h - hi.astype(F32)).astype(BF16)
    r1 = _dot(hi, wr_ref[...])
    r2 = _dot(lo, wr_ref[:, 0:LANES])
    work = r1[:, :LANES] + r1[:, LANES:] + r2 + br_ref[...]
    lane = lax.broadcasted_iota(jnp.int32, (tm, LANES), 1)
    vals, idxs, hots = [], [], []
    for _ in range(TOP_K):
        mk = jnp.max(work, axis=-1, keepdims=True)
        ik = jnp.min(jnp.where(work == mk, lane, LANES), axis=-1, keepdims=True)
        oh = lane == ik
        work = jnp.where(oh, -jnp.inf, work)
        vals.append(mk)
        idxs.append(ik)
        hots.append(oh)
    exps = [jnp.exp(v - vals[0]) for v in vals]
    denom = exps[0] + exps[1] + exps[2] + exps[3]
    chosen = jnp.zeros((tm, LANES), F32)
    for oh in hots:
        chosen = chosen + jnp.where(oh, 1.0, 0.0)
    r = lax.broadcasted_iota(jnp.int32, (tm, tm), 0)
    c = lax.broadcasted_iota(jnp.int32, (tm, tm), 1)
    below = jnp.where(c < r, 1.0, 0.0).astype(BF16)
    rank = _dot(below, chosen.astype(BF16)) + carry[0:1, :]
    meta = jnp.zeros((tm, LANES), jnp.int32)
    gate = jnp.zeros((tm, LANES), F32)
    for k in range(TOP_K):
        rk = jnp.sum(jnp.where(hots[k], rank, 0.0), axis=-1, keepdims=True).astype(jnp.int32)
        meta = jnp.where(lane == k, idxs[k], meta)
        meta = jnp.where(lane == TOP_K + k, rk, meta)
        gate = jnp.where(lane == k, exps[k] / denom, gate)
    meta_ref[...] = meta
    gate_ref[...] = gate
    total = carry[...] + jnp.sum(chosen, axis=0, keepdims=True)
    carry[...] = total
    cnt_ref[...] = total


def _router(x1p, x1s, nf, wr, br, tm):
    n_prompt, n_sample = x1p.shape[0], x1s.shape[0]
    n = n_prompt + n_sample
    npt = n_prompt // tm
    row = lambda w: pl.BlockSpec((tm, w), lambda i: (i, 0))
    return pl.pallas_call(
        functools.partial(_router_body, tm=tm, npt=npt),
        grid=(n // tm,),
        in_specs=[*_split_specs(tm, D_MODEL, npt), pl.BlockSpec((1, D_MODEL), lambda i: (0, 0)),
                  pl.BlockSpec((D_MODEL, 2 * LANES), lambda i: (0, 0)),
                  pl.BlockSpec((1, LANES), lambda i: (0, 0))],
        out_specs=[row(D_MODEL), row(LANES), row(LANES), pl.BlockSpec((8, LANES), lambda i: (0, 0))],
        out_shape=[jax.ShapeDtypeStruct((n, D_MODEL), F32),
                   jax.ShapeDtypeStruct((n, LANES), jnp.int32),
                   jax.ShapeDtypeStruct((n, LANES), F32),
                   jax.ShapeDtypeStruct((8, LANES), F32)],
        scratch_shapes=[pltpu.VMEM((8, LANES), F32)],
        compiler_params=_params(("arbitrary",)),
        name="router",
    )(x1p, x1s, nf, wr, br)


def _dispatch_body(slot_ref, last_ref, used_ref, h_ref, xs_out, zbuf, sem, zsem, *, tm):
    i = pl.program_id(0)
    tg = zbuf.shape[0]

    def zero_copy(e):
        return pltpu.make_async_copy(zbuf, xs_out.at[pl.ds(pl.multiple_of(last_ref[e], tg), tg)], zsem)

    @pl.when(i == 0)
    def _():
        zbuf[...] = jnp.zeros_like(zbuf)
        for e in range(2 * N_EXPERTS):
            @pl.when(used_ref[e] > 0)
            def _():
                zero_copy(e).start()
        for e in range(2 * N_EXPERTS):
            @pl.when(used_ref[e] > 0)
            def _():
                zero_copy(e).wait()

    base = i * tm * TOP_K

    def copy(r, k):
        s = slot_ref[base + r * TOP_K + k]
        return pltpu.make_async_copy(h_ref.at[pl.ds(r, 1)], xs_out.at[pl.ds(s, 1)], sem)

    for r in range(tm):
        for k in range(TOP_K):
            copy(r, k).start(priority=k % 2)
    for r in range(tm):
        for k in range(TOP_K):
            copy(r, k).wait()


def _dispatch(slots, last_rows, used, h, n_rows, tm):
    n = h.shape[0]
    grid_spec = pltpu.PrefetchScalarGridSpec(
        num_scalar_prefetch=3,
        grid=(n // tm,),
        in_specs=[pl.BlockSpec((tm, D_MODEL), lambda i, *_: (i, 0))],
        out_specs=pl.BlockSpec(memory_space=pl.ANY),
        scratch_shapes=[pltpu.VMEM((EXPERT_TILE, D_MODEL), F32),
                        pltpu.SemaphoreType.DMA(()), pltpu.SemaphoreType.DMA(())])
    return pl.pallas_call(
        functools.partial(_dispatch_body, tm=tm),
        grid_spec=grid_spec,
        out_shape=jax.ShapeDtypeStruct((n_rows, D_MODEL), F32),
        compiler_params=_params(("arbitrary",), has_side_effects=True, disable_bounds_checks=True),
        name="dispatch",
    )(slots, last_rows, used, h)


def _expert_body(te_ref, nt_ref, xs_ref, w1_ref, b1g_ref, b1l_ref, w2_ref, b2_ref, y_ref,
                 wt, w1g, w1l, w2s):
    i = pl.program_id(0)
    d_exp = w2_ref.shape[1]
    changed = jnp.logical_or(i == 0, te_ref[i] != te_ref[jnp.maximum(i - 1, 0)])

    @pl.when(jnp.logical_and(changed, i < nt_ref[0]))
    def _():
        chunk = 2 * LANES
        for c in range(2 * d_exp // chunk):
            t = w1_ref[0, :, chunk * c:chunk * (c + 1)].T
            for cb in range(D_MODEL // LANES):
                wt[cb, chunk * c:chunk * (c + 1), :] = t[:, LANES * cb:LANES * (cb + 1)]
        for cb in range(D_MODEL // LANES):
            cols = slice(LANES * cb, LANES * (cb + 1))
            w1g[:, cols] = wt[cb, pl.ds(0, d_exp, stride=2), :].astype(BF16)
            w1l[:, cols] = wt[cb, pl.ds(1, d_exp, stride=2), :].astype(BF16)
        w2s[...] = w2_ref[0].astype(BF16)

    @pl.when(i < nt_ref[0])
    def _():
        x = xs_ref[...].astype(BF16)
        glu = jnp.minimum(_dot_nt(x, w1g[...]) + b1g_ref[0], SWIGLU_LIMIT)
        lin = jnp.clip(_dot_nt(x, w1l[...]) + b1l_ref[0], -SWIGLU_LIMIT, SWIGLU_LIMIT)
        act = glu * jax.nn.sigmoid(SWIGLU_ALPHA * glu) * (lin + 1.0)
        y_ref[...] = _dot(act.astype(BF16), w2s[...]) + b2_ref[0]

    @pl.when(i >= nt_ref[0])
    def _():
        y_ref[...] = jnp.zeros_like(y_ref)


def _experts(tile_expert, n_tiles, xs, w1, b1g, b1l, w2, b2):
    nr = xs.shape[0]
    d_exp = w2.shape[1]
    tg = EXPERT_TILE
    live = lambda i, te, nt: (jnp.minimum(i, nt[0] - 1), 0)
    wspec = lambda a, b: pl.BlockSpec((1, a, b), lambda i, te, nt: (te[i], 0, 0))
    grid_spec = pltpu.PrefetchScalarGridSpec(
        num_scalar_prefetch=2,
        grid=(nr // tg,),
        in_specs=[pl.BlockSpec((tg, D_MODEL), live),
                  wspec(D_MODEL, 2 * d_exp), wspec(1, d_exp), wspec(1, d_exp),
                  wspec(d_exp, D_MODEL), wspec(1, D_MODEL)],
        out_specs=pl.BlockSpec((tg, D_MODEL), lambda i, te, nt: (i, 0)),
        scratch_shapes=[pltpu.VMEM((D_MODEL // LANES, 2 * d_exp, LANES), F32),
                        pltpu.VMEM((d_exp, D_MODEL), BF16),
                        pltpu.VMEM((d_exp, D_MODEL), BF16),
                        pltpu.VMEM((d_exp, D_MODEL), BF16)])
    return pl.pallas_call(
        _expert_body,
        grid_spec=grid_spec,
        out_shape=jax.ShapeDtypeStruct((nr, D_MODEL), F32),
        compiler_params=_params(("arbitrary",)),
        name="experts",
    )(tile_expert, n_tiles, xs, w1, b1g, b1l, w2, b2)


def _combine_body(slot_ref, xp_ref, xs_ref, gate_ref, pp_ref, ps_ref, np_ref, wg_ref, wp_ref,
                  nfin_ref, ys_hbm, op_ref, os_ref, buf, sem, *, tm, npt):
    i = pl.program_id(0)
    n = pl.num_programs(0)

    def copy(ii, sl, r, k):
        s = slot_ref[ii * tm * TOP_K + r * TOP_K + k]
        return pltpu.make_async_copy(ys_hbm.at[pl.ds(s, 1)], buf.at[sl, k, pl.ds(r, 1)], sem.at[sl])

    def issue(ii, sl):
        for r in range(tm):
            for k in range(TOP_K):
                copy(ii, sl, r, k).start(priority=k % 2)

    @pl.when(i == 0)
    def _():
        issue(0, 0)

    sl = lax.rem(i, 2)
    for parity in (0, 1):
        @pl.when(sl == parity)
        def _():
            @pl.when(i + 1 < n)
            def _():
                issue(i + 1, 1 - parity)

            for r in range(tm):
                for k in range(TOP_K):
                    copy(i, parity, r, k).wait()

    g = gate_ref[...]
    x2 = _pick(i, npt, xp_ref, xs_ref)
    for k in range(TOP_K):
        x2 = x2 + g[:, k:k + 1] * buf[sl, k]
    hn = _rms(x2, np_ref[...]).astype(BF16)
    emb = _dot(_pick(i, npt, pp_ref, ps_ref).astype(BF16), wp_ref[...])
    x3 = x2 + jax.nn.sigmoid(_dot(hn, wg_ref[...])) * emb
    _put(i, npt, op_ref, os_ref, _rms(x3, nfin_ref[...]))


def _combine(slots, x1p, x1s, gate, pp, ps, npl, wg, wp, nfin, ys, tm):
    n_prompt, n_sample = x1p.shape[0], x1s.shape[0]
    n = n_prompt + n_sample
    npt = n_prompt // tm
    ple = pp.shape[1]
    vec = pl.BlockSpec((1, D_MODEL), lambda i, s: (0, 0))
    grid_spec = pltpu.PrefetchScalarGridSpec(
        num_scalar_prefetch=1,
        grid=(n // tm,),
        in_specs=[*_split_specs(tm, D_MODEL, npt), pl.BlockSpec((tm, LANES), lambda i, s: (i, 0)),
                  *_split_specs(tm, ple, npt), vec,
                  pl.BlockSpec((D_MODEL, D_MODEL), lambda i, s: (0, 0)),
                  pl.BlockSpec((ple, D_MODEL), lambda i, s: (0, 0)), vec,
                  pl.BlockSpec(memory_space=pl.ANY)],
        out_specs=list(_split_specs(tm, D_MODEL, npt)),
        scratch_shapes=[pltpu.VMEM((2, TOP_K, tm, D_MODEL), F32), pltpu.SemaphoreType.DMA((2,))])
    return pl.pallas_call(
        functools.partial(_combine_body, tm=tm, npt=npt),
        grid_spec=grid_spec,
        out_shape=[jax.ShapeDtypeStruct((n_prompt, D_MODEL), F32),
                   jax.ShapeDtypeStruct((n_sample, D_MODEL), F32)],
        compiler_params=_params(("arbitrary",), disable_bounds_checks=True),
        name="combine",
    )(slots, x1p, x1s, gate, pp, ps, npl, wg, wp, nfin, ys)


def _token_tile(n_prompt, n_sample):
    for tm in (256, 128, 64, 32, 16, 8):
        if n_prompt % tm == 0 and n_sample % tm == 0:
            return tm
    raise ValueError("token counts must be multiples of 8")


def _layer(xp, xs, csp, css, pp, ps, cache_ckv, cache_kpe_t, state_conv, page_table, dims,
           norm_mix, w_in, w_dw, b_dw, ln_g, ln_b, w_pw2, b_pw2, norm_q, w_qb, norm_kv, w_kvb, w_o,
           norm_ffn, w_router, b_router, w_e1, b_e1, w_e2, b_e2, norm_ple, w_ple_gate, w_ple_proj,
           norm_out):
    batch, seq, dec_batch, s_len = dims
    n_prompt, n_sample = batch * seq, dec_batch * s_len
    n = n_prompt + n_sample
    tm = _token_tile(n_prompt, n_sample)
    vec = lambda a: a.reshape(1, -1)

    kpe_end = 2 * D_MODEL + Q_LORA + KV_LORA + QK_ROPE
    win = jnp.concatenate([w_in[:, :kpe_end], jnp.zeros((D_MODEL, _C_ZC - _C_KPE - QK_ROPE), F32),
                           w_in[:, kpe_end:]], axis=1).astype(BF16)
    wq3 = w_qb.reshape(Q_LORA, N_HEADS, QK_NOPE + QK_ROPE)
    wqb = jnp.concatenate([wq3[:, :, :QK_NOPE].reshape(Q_LORA, -1),
                           wq3[:, :, QK_NOPE:].reshape(Q_LORA, -1)], axis=1).astype(BF16)
    wkv3 = w_kvb.reshape(KV_LORA, N_HEADS, QK_NOPE + V_HEAD)
    wkv = jnp.concatenate([wkv3[:, :, :QK_NOPE].reshape(KV_LORA, -1),
                           wkv3[:, :, QK_NOPE:].reshape(KV_LORA, -1)], axis=1).astype(BF16)
    wuk_t = jnp.transpose(wkv3[:, :, :QK_NOPE], (1, 2, 0)).astype(BF16)
    wuv_h = jnp.transpose(wkv3[:, :, QK_NOPE:], (1, 0, 2)).astype(BF16)
    wdw = jnp.broadcast_to(w_dw[:, None, :], (CONV_W, SUBLANES, D_MODEL))

    v_p, v_s, q_cat, ckv_p, ckv_s, kpe_p, kpe_s, zc, za = _inproj(
        xp, xs, csp, css, vec(norm_mix), win, vec(norm_q), wqb, vec(norm_kv), tm)

    k_cat, v_h = _prompt_kv(ckv_p, kpe_p, wkv, tm)
    tq = 512 if seq % 512 == 0 else seq
    attn_p = _prompt_attention(q_cat, k_cat, v_h, batch, seq, tq)

    qs = _sample_q_absorb(q_cat, wuk_t, n_prompt, n_sample)
    o_lat = _sample_attention(page_table, qs, ckv_s, kpe_s, cache_ckv, cache_kpe_t, s_len)
    attn_s = _sample_o_expand(o_lat, wuv_h)

    conv_w = (wdw, vec(b_dw), vec(ln_g), vec(ln_b), w_pw2.astype(BF16), vec(b_pw2), w_o.astype(BF16))
    tc = 256 if seq % 256 == 0 else seq
    x1_p = _convmerge_prompt(v_p, zc, za, attn_p, xp, *conv_w, batch, seq, tc)
    vext = jnp.concatenate([state_conv, v_s.reshape(dec_batch, s_len, D_MODEL)], axis=1)
    g = max(1, min(dec_batch, 256 // s_len))
    x1_s = _convmerge_sample(vext, zc, za, attn_s, xs, *conv_w, n_prompt, g)

    wr_hi = w_router.astype(BF16)
    wr_lo = (w_router - wr_hi.astype(F32)).astype(BF16)
    padw = jnp.zeros((D_MODEL, LANES - N_EXPERTS), BF16)
    wr = jnp.concatenate([wr_hi, padw, wr_lo, padw], axis=1)
    br = jnp.concatenate([b_router, jnp.full((LANES - N_EXPERTS,), NEG, F32)]).reshape(1, LANES)
    h, meta, gate, cnt = _router(x1_p, x1_s, vec(norm_ffn), wr, br, tm)

    tg = EXPERT_TILE
    counts = cnt[0, :N_EXPERTS].astype(jnp.int32)
    tiles_per = (counts + tg - 1) // tg
    tile_end = jnp.cumsum(tiles_per)
    starts = (tile_end - tiles_per) * tg
    n_tiles_max = (n * TOP_K + N_EXPERTS * (tg - 1)) // tg
    picked = meta[:, :TOP_K, None] == jnp.arange(N_EXPERTS, dtype=jnp.int32)
    slots = (jnp.sum(jnp.where(picked, starts, 0), axis=-1) + meta[:, TOP_K:2 * TOP_K]).reshape(-1)
    tile_ids = jnp.arange(n_tiles_max, dtype=jnp.int32)
    tile_expert = jnp.minimum(jnp.sum((tile_end[None, :] <= tile_ids[:, None]).astype(jnp.int32), axis=1),
                              N_EXPERTS - 1)
    n_tiles = tile_end[-1:].astype(jnp.int32)
    tail = n_tiles[0] + jnp.arange(N_EXPERTS, dtype=jnp.int32)
    clear_rows = jnp.concatenate([jnp.maximum(tile_end - 1, 0), jnp.minimum(tail, n_tiles_max - 1)]) * tg
    clear_used = jnp.concatenate([tiles_per, (tail < n_tiles_max).astype(jnp.int32)])

    xs_rows = _dispatch(slots, clear_rows, clear_used, h, n_tiles_max * tg, tm)
    ys = _experts(tile_expert, n_tiles, xs_rows, w_e1, b_e1[:, None, 0::2], b_e1[:, None, 1::2],
                  w_e2, b_e2[:, None, :])

    y_p, y_s = _combine(slots, x1_p, x1_s, gate, pp, ps, vec(norm_ple), w_ple_gate.astype(BF16),
                        w_ple_proj.astype(BF16), vec(norm_out), ys, tm)
    conv_p = v_p.reshape(batch, seq, D_MODEL)[:, seq - (CONV_W - 1):]
    conv_s = vext[:, s_len:]
    return y_p, y_s, ckv_p, ckv_s, kpe_p, kpe_s, conv_p, conv_s


def kernel(x_prompt, x_sample, cache_ckv, cache_kpe, state_conv, page_table, p_prompt, p_sample, norm_mix, w_in, w_dw, b_dw, ln_conv_g, ln_conv_b, w_pw2, b_pw2, norm_q, w_qb, norm_kv, w_kvb, w_o, norm_ffn, w_router, b_router, w_e1, b_e1, w_e2, b_e2, norm_ple, w_ple_gate, w_ple_proj, norm_final):
    depth = w_in.shape[0]
    batch, seq, _ = x_prompt.shape
    dec_batch, s_len, _ = x_sample.shape
    n_prompt, n_sample = batch * seq, dec_batch * s_len
    assert depth == 1 and n_prompt % n_sample == 0 and seq >= HALO and s_len <= SUBLANES
    tm = _token_tile(n_prompt, n_sample)
    assert seq % tm == 0 and tm % s_len == 0
    past_len = page_table.shape[1] * cache_ckv.shape[2]

    half = QK_ROPE // 2
    inv_freq = ROPE_THETA ** (-jnp.arange(half, dtype=F32) / half)
    pos = jnp.concatenate([jnp.arange(seq), past_len + jnp.tile(jnp.arange(s_len), tm // s_len)])
    ang = pos.astype(F32)[:, None] * inv_freq[None, :]
    cos, sin = jnp.cos(ang), jnp.sin(ang)
    cs = jnp.concatenate([cos, cos, cos, cos, -sin, sin, -sin, sin], axis=1)

    i = 0
    y_p, y_s, ckv_p, ckv_s, kpe_p, kpe_s, conv_p, conv_s = _layer(
        x_prompt.reshape(n_prompt, D_MODEL), x_sample.reshape(n_sample, D_MODEL), cs[:seq], cs[seq:],
        p_prompt[i].reshape(n_prompt, -1), p_sample[i].reshape(n_sample, -1),
        cache_ckv[i], jnp.swapaxes(cache_kpe[i], 1, 2), state_conv[i], page_table,
        (batch, seq, dec_batch, s_len),
        norm_mix[i], w_in[i], w_dw[i], b_dw[i], ln_conv_g[i], ln_conv_b[i], w_pw2[i], b_pw2[i],
        norm_q[i], w_qb[i], norm_kv[i], w_kvb[i], w_o[i], norm_ffn[i], w_router[i], b_router[i],
        w_e1[i], b_e1[i], w_e2[i], b_e2[i], norm_ple[i], w_ple_gate[i], w_ple_proj[i], norm_final)
    return (y_p.reshape(batch, seq, D_MODEL), y_s.reshape(dec_batch, s_len, D_MODEL),
            ckv_p.reshape(1, batch, seq, KV_LORA), kpe_p.reshape(1, batch, seq, QK_ROPE), conv_p[None],
            ckv_s.reshape(1, dec_batch, s_len, KV_LORA), kpe_s.reshape(1, dec_batch, s_len, QK_ROPE),
            conv_s[None])
```

```python
import functools

import jax
import jax.numpy as jnp
from jax import lax
from jax.experimental import pallas as pl
from jax.experimental.pallas import tpu as pltpu

F32 = jnp.float32
BF16 = jnp.bfloat16

D_MODEL = 1024
N_HEADS = 8
QK_NOPE = 128
QK_ROPE = 64
V_HEAD = 128
Q_LORA = 512
KV_LORA = 512
N_EXPERTS = 32
TOP_K = 4
CONV_W = 31
ROPE_THETA = 10000.0
ATTN_SCALE = (QK_NOPE + QK_ROPE) ** -0.5
SWIGLU_LIMIT = 7.0
SWIGLU_ALPHA = 1.702
EPS = 1e-6
NEG = -1e30

LANES = 128
SUBLANES = 8
HALO = 32
EXPERT_TILE = 256
ATTN_HEADS_PER_STEP = 2
VMEM_LIMIT = 56 * 1024 * 1024

_C_GLU_A, _C_GLU_G, _C_Q, _C_KV, _C_KPE, _C_ZC, _C_ZA, _C_END = (
    0, 1024, 2048, 2560, 3072, 3200, 4224, 5248)


def _params(sem, **kw):
    return pltpu.CompilerParams(dimension_semantics=sem, vmem_limit_bytes=VMEM_LIMIT, **kw)


def _rms(x, g):
    return x * lax.rsqrt(jnp.mean(x * x, axis=-1, keepdims=True) + EPS) * g


def _dot(a, b):
    return jnp.dot(a, b, preferred_element_type=F32)


def _dot_nt(a, b):
    return lax.dot_general(a, b, (((1,), (1,)), ((), ())), preferred_element_type=F32)


def _rope128(x, cos, sin):
    lane = lax.broadcasted_iota(jnp.int32, x.shape, 1)
    rot = jnp.where((lane % QK_ROPE) < QK_ROPE // 2,
                    pltpu.roll(x, LANES - QK_ROPE // 2, 1), pltpu.roll(x, QK_ROPE // 2, 1))
    return x * cos + rot * sin


def _split_specs(tm, width, npt):
    prompt = pl.BlockSpec((tm, width), lambda i, *_: (jnp.minimum(i, npt - 1), 0))
    sample = pl.BlockSpec((tm, width), lambda i, *_: (jnp.maximum(i - npt, 0), 0))
    return prompt, sample


def _pick(i, npt, p_ref, s_ref):
    return jnp.where(i < npt, p_ref[...], s_ref[...])


def _put(i, npt, p_ref, s_ref, val):
    @pl.when(i < npt)
    def _():
        p_ref[...] = val

    @pl.when(i >= npt)
    def _():
        s_ref[...] = val


def _inproj_body(xp_ref, xs_ref, csp_ref, css_ref, nmix_ref, win_ref, nq_ref, wqb_ref, nkv_ref,
                 vp_ref, vs_ref, q_ref, ckvp_ref, ckvs_ref, kpep_ref, kpes_ref, zc_ref, za_ref,
                 *, npt):
    i = pl.program_id(0)
    h = _rms(_pick(i, npt, xp_ref, xs_ref), nmix_ref[...]).astype(BF16)
    cs = _pick(i, npt, csp_ref, css_ref)
    cos = cs[:, 0:LANES]
    sin = cs[:, LANES:2 * LANES]

    def proj(lo, hi):
        return _dot(h, win_ref[:, lo:hi])

    _put(i, npt, vp_ref, vs_ref, proj(_C_GLU_A, _C_GLU_G) * jax.nn.sigmoid(proj(_C_GLU_G, _C_Q)))
    qn = _rms(proj(_C_Q, _C_KV), nq_ref[...]).astype(BF16)
    _put(i, npt, ckvp_ref, ckvs_ref, _rms(proj(_C_KV, _C_KPE), nkv_ref[...]))
    _put(i, npt, kpep_ref, kpes_ref, _rope128(proj(_C_KPE, _C_ZC), cos, sin)[:, :QK_ROPE])
    zc_ref[...] = jax.nn.sigmoid(proj(_C_ZC, _C_ZA))
    za_ref[...] = jax.nn.sigmoid(proj(_C_ZA, _C_END))
    q = _dot(qn, wqb_ref[...])
    nope_w = N_HEADS * QK_NOPE
    for j in range(N_HEADS // 2):
        pe = _rope128(q[:, nope_w + LANES * j:nope_w + LANES * (j + 1)], cos, sin)
        for hh in (2 * j, 2 * j + 1):
            q_ref[hh, :, 0:QK_NOPE] = q[:, QK_NOPE * hh:QK_NOPE * (hh + 1)].astype(BF16)
            lo = QK_ROPE * (hh % 2)
            q_ref[hh, :, QK_NOPE:QK_NOPE + QK_ROPE] = pe[:, lo:lo + QK_ROPE].astype(BF16)


def _inproj(xp, xs, csp, css, nmix, win, nq, wqb, nkv, tm):
    n_prompt, n_sample = xp.shape[0], xs.shape[0]
    n = n_prompt + n_sample
    npt = n_prompt // tm
    per_seq = csp.shape[0] // tm
    row = lambda w: pl.BlockSpec((tm, w), lambda i: (i, 0))
    full = lambda a, b: pl.BlockSpec((a, b), lambda i: (0, 0))
    split = lambda w: _split_specs(tm, w, npt)
    sds = jax.ShapeDtypeStruct
    return pl.pallas_call(
        functools.partial(_inproj_body, npt=npt),
        grid=(n // tm,),
        in_specs=[*split(D_MODEL),
                  pl.BlockSpec((tm, 2 * LANES), lambda i: (i % per_seq, 0)), full(tm, 2 * LANES),
                  full(1, D_MODEL), full(D_MODEL, win.shape[1]),
                  full(1, Q_LORA), full(Q_LORA, wqb.shape[1]), full(1, KV_LORA)],
        out_specs=[*split(D_MODEL),
                   pl.BlockSpec((N_HEADS, tm, QK_NOPE + QK_ROPE), lambda i: (0, i, 0)),
                   *split(KV_LORA), *split(QK_ROPE), row(D_MODEL), row(D_MODEL)],
        out_shape=[sds((n_prompt, D_MODEL), F32), sds((n_sample, D_MODEL), F32),
                   sds((N_HEADS, n, QK_NOPE + QK_ROPE), BF16),
                   sds((n_prompt, KV_LORA), F32), sds((n_sample, KV_LORA), F32),
                   sds((n_prompt, QK_ROPE), F32), sds((n_sample, QK_ROPE), F32),
                   sds((n, D_MODEL), F32), sds((n, D_MODEL), F32)],
        compiler_params=_params(("arbitrary",)),
        name="inproj",
    )(xp, xs, csp, css, nmix, win, nq, wqb, nkv)


def _kv_body(ckv_ref, kpe_ref, wkv_ref, k_ref, v_ref):
    kv = _dot(ckv_ref[...].astype(BF16), wkv_ref[...])
    kp = kpe_ref[...].astype(BF16)
    for hh in range(N_HEADS):
        k_ref[hh, :, 0:QK_NOPE] = kv[:, QK_NOPE * hh:QK_NOPE * (hh + 1)].astype(BF16)
        k_ref[hh, :, QK_NOPE:QK_NOPE + QK_ROPE] = kp
        lo = N_HEADS * QK_NOPE + V_HEAD * hh
        v_ref[hh] = kv[:, lo:lo + V_HEAD].astype(BF16)


def _prompt_kv(ckv, kpe, wkv, tm):
    n_prompt = ckv.shape[0]
    return pl.pallas_call(
        _kv_body,
        grid=(n_prompt // tm,),
        in_specs=[pl.BlockSpec((tm, KV_LORA), lambda i: (i, 0)),
                  pl.BlockSpec((tm, QK_ROPE), lambda i: (i, 0)),
                  pl.BlockSpec(wkv.shape, lambda i: (0, 0))],
        out_specs=[pl.BlockSpec((N_HEADS, tm, QK_NOPE + QK_ROPE), lambda i: (0, i, 0)),
                   pl.BlockSpec((N_HEADS, tm, V_HEAD), lambda i: (0, i, 0))],
        out_shape=[jax.ShapeDtypeStruct((N_HEADS, n_prompt, QK_NOPE + QK_ROPE), BF16),
                   jax.ShapeDtypeStruct((N_HEADS, n_prompt, V_HEAD), BF16)],
        compiler_params=_params(("arbitrary",)),
        name="prompt_kv",
    )(ckv, kpe, wkv)


def _attn_body(q_ref, k_ref, v_ref, o_ref, *, tq):
    qi = pl.program_id(2)
    heads = q_ref.shape[0]
    row = lax.broadcasted_iota(jnp.int32, (tq, tq), 0)
    col = lax.broadcasted_iota(jnp.int32, (tq, tq), 1)

    def step(j, carry, diagonal):
        off = pl.multiple_of(j * tq, tq)
        out = []
        for hh in range(heads):
            m, l, acc = carry[hh]
            k = k_ref[hh, pl.ds(off, tq), :]
            v = v_ref[hh, pl.ds(off, tq), :]
            s = _dot_nt(q_ref[hh], k) * ATTN_SCALE
            if diagonal:
                s = jnp.where(col <= row, s, NEG)
            m_new = jnp.maximum(m, jnp.max(s, axis=-1, keepdims=True))
            corr = jnp.exp(m - m_new)
            p = jnp.exp(s - m_new)
            l = l * corr + jnp.sum(p, axis=-1, keepdims=True)
            acc = acc * corr + _dot(p.astype(BF16), v)
            out.append((m_new, l, acc))
        return tuple(out)

    init = tuple((jnp.full((tq, 1), NEG, F32), jnp.zeros((tq, 1), F32),
                  jnp.zeros((tq, V_HEAD), F32)) for _ in range(heads))
    carry = lax.fori_loop(0, qi, lambda j, c: step(j, c, False), init)
    carry = step(qi, carry, True)
    for hh in range(heads):
        _, l, acc = carry[hh]
        o_ref[:, V_HEAD * hh:V_HEAD * (hh + 1)] = acc / l


def _prompt_attention(q_cat, k_cat, v_h, batch, seq, tq):
    nq = seq // tq
    n_prompt = batch * seq
    hp = ATTN_HEADS_PER_STEP
    return pl.pallas_call(
        functools.partial(_attn_body, tq=tq),
        grid=(batch, N_HEADS // hp, nq),
        in_specs=[pl.BlockSpec((hp, tq, QK_NOPE + QK_ROPE), lambda b, h, i: (h, b * nq + i, 0)),
                  pl.BlockSpec((hp, seq, QK_NOPE + QK_ROPE), lambda b, h, i: (h, b, 0)),
                  pl.BlockSpec((hp, seq, V_HEAD), lambda b, h, i: (h, b, 0))],
        out_specs=pl.BlockSpec((tq, hp * V_HEAD), lambda b, h, i: (b * nq + i, h)),
        out_shape=jax.ShapeDtypeStruct((n_prompt, N_HEADS * V_HEAD), F32),
        compiler_params=_params(("arbitrary", "arbitrary", "arbitrary")),
        name="prompt_attention",
    )(q_cat, k_cat, v_h)


def _qabs_body(q_ref, wuk_ref, o_ref):
    q = q_ref[0]
    o_ref[0, :, 0:KV_LORA] = _dot(q[:, :QK_NOPE], wuk_ref[0])
    o_ref[0, :, KV_LORA:KV_LORA + QK_ROPE] = q[:, QK_NOPE:].astype(F32)


def _sample_q_absorb(q_cat, wuk_t, n_prompt, n_sample):
    blk = n_prompt // n_sample
    return pl.pallas_call(
        _qabs_body,
        grid=(N_HEADS,),
        in_specs=[pl.BlockSpec((1, n_sample, QK_NOPE + QK_ROPE), lambda h: (h, blk, 0)),
                  pl.BlockSpec((1, QK_NOPE, KV_LORA), lambda h: (h, 0, 0))],
        out_specs=pl.BlockSpec((1, n_sample, KV_LORA + QK_ROPE), lambda h: (h, 0, 0)),
        out_shape=jax.ShapeDtypeStruct((N_HEADS, n_sample, KV_LORA + QK_ROPE), F32),
        compiler_params=_params(("arbitrary",)),
        name="sample_q_absorb",
    )(q_cat, wuk_t)


def _oabs_body(o_ref, wuv_ref, out_ref):
    out_ref[...] = _dot(o_ref[0].astype(BF16), wuv_ref[0])


def _sample_o_expand(o_lat, wuv_h):
    n_sample = o_lat.shape[1]
    return pl.pallas_call(
        _oabs_body,
        grid=(N_HEADS,),
        in_specs=[pl.BlockSpec((1, n_sample, KV_LORA), lambda h: (h, 0, 0)),
                  pl.BlockSpec((1, KV_LORA, V_HEAD), lambda h: (h, 0, 0))],
        out_specs=pl.BlockSpec((n_sample, V_HEAD), lambda h: (0, h)),
        out_shape=jax.ShapeDtypeStruct((n_sample, N_HEADS * V_HEAD), F32),
        compiler_params=_params(("arbitrary",)),
        name="sample_o_expand",
    )(o_lat, wuv_h)


def _decode_body(pt_ref, qs_ref, ckvn_ref, kpen_ref, cc_hbm, ckt_hbm, o_ref,
                 cbuf, kbuf, sem, *, n_pages, cp, page, s_len):
    b = pl.program_id(0)
    nb = pl.num_programs(0)
    nchunk = n_pages // cp
    rows = N_HEADS * s_len

    def copies(bb, j, slot):
        out = []
        for p in range(cp):
            pg = pt_ref[bb * n_pages + j * cp + p]
            out.append(pltpu.make_async_copy(cc_hbm.at[pg], cbuf.at[slot, p], sem.at[0, slot]))
            out.append(pltpu.make_async_copy(ckt_hbm.at[pg], kbuf.at[slot, p], sem.at[1, slot]))
        return out

    def start(bb, j, slot):
        for c in copies(bb, j, slot):
            c.start()

    nslot = cbuf.shape[0]
    ahead = nslot - 1

    @pl.when(b == 0)
    def _():
        for t in range(ahead):
            start(0, t, t)

    q = qs_ref[...].reshape(rows, KV_LORA + QK_ROPE)
    ql = q[:, :KV_LORA].astype(BF16)
    qp = q[:, KV_LORA:].astype(BF16)

    def accumulate(carry, s, ck):
        m, l, acc = carry
        m_new = jnp.maximum(m, jnp.max(s, axis=-1, keepdims=True))
        corr = jnp.exp(m - m_new)
        p = jnp.exp(s - m_new)
        l = l * corr + jnp.sum(p, axis=-1, keepdims=True)
        acc = acc * corr + _dot(p.astype(BF16), ck)
        return m_new, l, acc

    def body(j, carry):
        g = b * nchunk + j
        slot = lax.rem(g, nslot)
        nxt = lax.rem(g + ahead, nslot)

        @pl.when(j + ahead < nchunk)
        def _():
            start(b, j + ahead, nxt)

        @pl.when(jnp.logical_and(j + ahead >= nchunk, b + 1 < nb))
        def _():
            start(b + 1, j + ahead - nchunk, nxt)

        for c in copies(b, j, slot):
            c.wait()
        ck = cbuf[slot].reshape(cp * page, KV_LORA).astype(BF16)
        kt = jnp.concatenate([kbuf[slot, p] for p in range(cp)], axis=1).astype(BF16)
        s = (_dot_nt(ql, ck) + _dot(qp, kt)) * ATTN_SCALE
        return accumulate(carry, s, ck)

    init = (jnp.full((rows, 1), NEG, F32), jnp.zeros((rows, 1), F32),
            jnp.zeros((rows, KV_LORA), F32))
    carry = lax.fori_loop(0, nchunk, body, init)

    pad = 16 - s_len
    cn = jnp.concatenate([ckvn_ref[...], jnp.zeros((pad, KV_LORA), F32)], axis=0).astype(BF16)
    kn = jnp.concatenate([kpen_ref[...], jnp.zeros((pad, QK_ROPE), F32)], axis=0).astype(BF16)
    s = (_dot_nt(ql, cn) + _dot_nt(qp, kn)) * ATTN_SCALE
    r = lax.broadcasted_iota(jnp.int32, s.shape, 0)
    c = lax.broadcasted_iota(jnp.int32, s.shape, 1)
    s = jnp.where(c <= r % s_len, s, NEG)
    _, l, acc = accumulate(carry, s, cn)
    o_ref[...] = (acc / l).reshape(N_HEADS, s_len, KV_LORA)


def _sample_attention(page_table, qs, ckv_s, kpe_s, cache_ckv, cache_kpe_t, s_len):
    dec_batch, n_pages = page_table.shape
    page = cache_ckv.shape[1]
    cp = next(c for c in (32, 8, 1) if n_pages % c == 0)
    nslot = min(3, n_pages // cp + 1)
    grid_spec = pltpu.PrefetchScalarGridSpec(
        num_scalar_prefetch=1,
        grid=(dec_batch,),
        in_specs=[pl.BlockSpec((N_HEADS, s_len, KV_LORA + QK_ROPE), lambda b, pt: (0, b, 0)),
                  pl.BlockSpec((s_len, KV_LORA), lambda b, pt: (b, 0)),
                  pl.BlockSpec((s_len, QK_ROPE), lambda b, pt: (b, 0)),
                  pl.BlockSpec(memory_space=pl.ANY),
                  pl.BlockSpec(memory_space=pl.ANY)],
        out_specs=pl.BlockSpec((N_HEADS, s_len, KV_LORA), lambda b, pt: (0, b, 0)),
        scratch_shapes=[pltpu.VMEM((nslot, cp, page, KV_LORA), F32),
                        pltpu.VMEM((nslot, cp, QK_ROPE, page), F32),
                        pltpu.SemaphoreType.DMA((2, nslot))])
    return pl.pallas_call(
        functools.partial(_decode_body, n_pages=n_pages, cp=cp, page=page, s_len=s_len),
        grid_spec=grid_spec,
        out_shape=jax.ShapeDtypeStruct((N_HEADS, dec_batch * s_len, KV_LORA), F32),
        compiler_params=_params(("arbitrary",)),
        name="sample_attention",
    )(page_table.reshape(-1), qs, ckv_s, kpe_s, cache_ckv, cache_kpe_t)


def _merge(c, zc, za, at, x, lg, lb, wpw, bpw, wo):
    mu = jnp.mean(c, axis=-1, keepdims=True)
    cc = c - mu
    var = jnp.mean(cc * cc, axis=-1, keepdims=True)
    y = cc * lax.rsqrt(var + EPS) * lg + lb
    a = (y * jax.nn.sigmoid(y)).astype(BF16)
    conv_out = _dot(a, wpw) + bpw
    merged = (zc * conv_out + za * at).astype(BF16)
    return x + _dot(merged, wo)


def _convmerge_prompt_body(vc_ref, vp_ref, zc_ref, za_ref, at_ref, x_ref, wdw_ref, bdw_ref,
                           lg_ref, lb_ref, wpw_ref, bpw_ref, wo_ref, o_ref, vext, cbuf, zs, *, tc, rb):
    t = pl.program_id(1)
    blk = vext.shape[1] - HALO
    vext[0, 0:HALO, :] = jnp.where(t == 0, 0.0, vp_ref[...])
    for i in range(tc // blk):
        if i:
            vext[i, 0:HALO, :] = vc_ref[i * blk - HALO:i * blk, :]
        vext[i, HALO:HALO + blk, :] = vc_ref[i * blk:(i + 1) * blk, :]
    lead = HALO - (CONV_W - 1)

    def rows(i, _):
        for s in range(rb):
            taps = [(a, rb * a + s - lead) for a in range((lead + CONV_W - 1 - s) // rb + 1)
                    if 0 <= rb * a + s - lead < CONV_W]
            for q in range(0, blk + (rb if s else 0), rb):
                acc = None
                for a, j in taps:
                    term = wdw_ref[j] * vext[i, q + rb * a:q + rb * (a + 1), :]
                    acc = term if acc is None else acc + term
                zs[s, q:q + rb, :] = acc
        for q in range(0, blk, rb):
            acc = jnp.broadcast_to(bdw_ref[...], (rb, D_MODEL))
            for s in range(rb):
                acc = acc + zs[s, q + s:q + s + rb, :]
            cbuf[pl.ds(pl.multiple_of(i * blk + q, rb), rb), :] = acc
        return 0

    lax.fori_loop(0, tc // blk, rows, 0)
    o_ref[...] = _merge(cbuf[...], zc_ref[...], za_ref[...], at_ref[...], x_ref[...],
                        lg_ref[...], lb_ref[...], wpw_ref[...], bpw_ref[...], wo_ref[...])


def _convmerge_prompt(v, zc, za, attn, x, wdw, bdw, lg, lb, wpw, bpw, wo, batch, seq, tc):
    nt = seq // tc
    per = tc // HALO
    blk = 64 if tc % 64 == 0 else tc
    n_prompt = batch * seq
    cur = pl.BlockSpec((tc, D_MODEL), lambda b, t: (b * nt + t, 0))
    prev = pl.BlockSpec((HALO, D_MODEL), lambda b, t: (jnp.maximum((b * nt + t) * per - 1, 0), 0))
    vec = pl.BlockSpec((1, D_MODEL), lambda b, t: (0, 0))
    mat = pl.BlockSpec((D_MODEL, D_MODEL), lambda b, t: (0, 0))
    return pl.pallas_call(
        functools.partial(_convmerge_prompt_body, tc=tc, rb=SUBLANES),
        grid=(batch, nt),
        in_specs=[cur, prev, cur, cur, cur, cur,
                  pl.BlockSpec(wdw.shape, lambda b, t: (0, 0, 0)), vec, vec, vec, mat, vec, mat],
        out_specs=cur,
        out_shape=jax.ShapeDtypeStruct((n_prompt, D_MODEL), F32),
        scratch_shapes=[pltpu.VMEM((tc // blk, HALO + blk, D_MODEL), F32),
                        pltpu.VMEM((tc, D_MODEL), F32),
                        pltpu.VMEM((SUBLANES, blk + SUBLANES, D_MODEL), F32)],
        compiler_params=_params(("arbitrary", "arbitrary")),
        name="convmerge_prompt",
    )(v, v, zc, za, attn, x, wdw, bdw, lg, lb, wpw, bpw, wo)


def _convmerge_sample_body(ve_ref, zc_ref, za_ref, at_ref, x_ref, wdw_ref, bdw_ref,
                           lg_ref, lb_ref, wpw_ref, bpw_ref, wo_ref, o_ref, cbuf, *, g, s_len):
    def one(i, _):
        acc = jnp.broadcast_to(bdw_ref[...], (s_len, D_MODEL))
        for j in range(CONV_W):
            acc = acc + wdw_ref[j, 0:s_len, :] * ve_ref[i, j:j + s_len, :]
        cbuf[pl.ds(pl.multiple_of(i * s_len, s_len), s_len), :] = acc
        return 0

    lax.fori_loop(0, g, one, 0)
    o_ref[...] = _merge(cbuf[...], zc_ref[...], za_ref[...], at_ref[...], x_ref[...],
                        lg_ref[...], lb_ref[...], wpw_ref[...], bpw_ref[...], wo_ref[...])


def _convmerge_sample(vext, zc, za, attn_s, x_s, wdw, bdw, lg, lb, wpw, bpw, wo, n_prompt, g):
    dec_batch, ext, _ = vext.shape
    s_len = ext - (CONV_W - 1)
    rows = g * s_len
    first = n_prompt // rows
    tok = pl.BlockSpec((rows, D_MODEL), lambda i: (first + i, 0))
    loc = pl.BlockSpec((rows, D_MODEL), lambda i: (i, 0))
    vec = pl.BlockSpec((1, D_MODEL), lambda i: (0, 0))
    mat = pl.BlockSpec((D_MODEL, D_MODEL), lambda i: (0, 0))
    return pl.pallas_call(
        functools.partial(_convmerge_sample_body, g=g, s_len=s_len),
        grid=(dec_batch // g,),
        in_specs=[pl.BlockSpec((g, ext, D_MODEL), lambda i: (i, 0, 0)), tok, tok, loc, loc,
                  pl.BlockSpec(wdw.shape, lambda i: (0, 0, 0)), vec, vec, vec, mat, vec, mat],
        out_specs=loc,
        out_shape=jax.ShapeDtypeStruct((dec_batch * s_len, D_MODEL), F32),
        scratch_shapes=[pltpu.VMEM((rows, D_MODEL), F32)],
        compiler_params=_params(("arbitrary",)),
        name="convmerge_sample",
    )(vext, zc, za, attn_s, x_s, wdw, bdw, lg, lb, wpw, bpw, wo)


def _router_body(xp_ref, xs_ref, nf_ref, wr_ref, br_ref, h_ref, meta_ref, gate_ref, cnt_ref, carry,
                 *, tm, npt):
    i = pl.program_id(0)

    @pl.when(i == 0)
    def _():
        carry[...] = jnp.zeros_like(carry)

    h = _rms(_pick(i, npt, xp_ref, xs_ref), nf_ref[...])
    h_ref[...] = h
    hi = h.astype(BF16)
    lo = (h - hi.astype(F32)).astype(BF16)
    r1 = _dot(hi, wr_ref[...])
    r2 = _dot(lo, wr_ref[:, 0:LANES])
    work = r1[:, :LANES] + r1[:, LANES:] + r2 + br_ref[...]
    lane = lax.broadcasted_iota(jnp.int32, (tm, LANES), 1)
    vals, idxs, hots = [], [], []
    for _ in range(TOP_K):
        mk = jnp.max(work, axis=-1, keepdims=True)
        ik = jnp.min(jnp.where(work == mk, lane, LANES), axis=-1, keepdims=True)
        oh = lane == ik
        work = jnp.where(oh, -jnp.inf, work)
        vals.append(mk)
        idxs.append(ik)
        hots.append(oh)
    exps = [jnp.exp(v - vals[0]) for v in vals]
    denom = exps[0] + exps[1] + exps[2] + exps[3]
    chosen = jnp.zeros((tm, LANES), F32)
    for oh in hots:
        chosen = chosen + jnp.where(oh, 1.0, 0.0)
    r = lax.broadcasted_iota(jnp.int32, (tm, tm), 0)
    c = lax.broadcasted_iota(jnp.int32, (tm, tm), 1)
    below = jnp.where(c < r, 1.0, 0.0).astype(BF16)
    rank = _dot(below, chosen.astype(BF16)) + carry[0:1, :]
    meta = jnp.zeros((tm, LANES), jnp.int32)
    gate = jnp.zeros((tm, LANES), F32)
    for k in range(TOP_K):
        rk = jnp.sum(jnp.where(hots[k], rank, 0.0), axis=-1, keepdims=True).astype(jnp.int32)
        meta = jnp.where(lane == k, idxs[k], meta)
        meta = jnp.where(lane == TOP_K + k, rk, meta)
        gate = jnp.where(lane == k, exps[k] / denom, gate)
    meta_ref[...] = meta
    gate_ref[...] = gate
    total = carry[...] + jnp.sum(chosen, axis=0, keepdims=True)
    carry[...] = total
    cnt_ref[...] = total


def _router(x1p, x1s, nf, wr, br, tm):
    n_prompt, n_sample = x1p.shape[0], x1s.shape[0]
    n = n_prompt + n_sample
    npt = n_prompt // tm
    row = lambda w: pl.BlockSpec((tm, w), lambda i: (i, 0))
    return pl.pallas_call(
        functools.partial(_router_body, tm=tm, npt=npt),
        grid=(n // tm,),
        in_specs=[*_split_specs(tm, D_MODEL, npt), pl.BlockSpec((1, D_MODEL), lambda i: (0, 0)),
                  pl.BlockSpec((D_MODEL, 2 * LANES), lambda i: (0, 0)),
                  pl.BlockSpec((1, LANES), lambda i: (0, 0))],
        out_specs=[row(D_MODEL), row(LANES), row(LANES), pl.BlockSpec((8, LANES), lambda i: (0, 0))],
        out_shape=[jax.ShapeDtypeStruct((n, D_MODEL), F32),
                   jax.ShapeDtypeStruct((n, LANES), jnp.int32),
                   jax.ShapeDtypeStruct((n, LANES), F32),
                   jax.ShapeDtypeStruct((8, LANES), F32)],
        scratch_shapes=[pltpu.VMEM((8, LANES), F32)],
        compiler_params=_params(("arbitrary",)),
        name="router",
    )(x1p, x1s, nf, wr, br)


def _dispatch_body(slot_ref, last_ref, used_ref, h_ref, xs_out, zbuf, sem, zsem, *, tm):
    i = pl.program_id(0)
    tg = zbuf.shape[0]

    def zero_copy(e):
        return pltpu.make_async_copy(zbuf, xs_out.at[pl.ds(pl.multiple_of(last_ref[e], tg), tg)], zsem)

    @pl.when(i == 0)
    def _():
        zbuf[...] = jnp.zeros_like(zbuf)
        for e in range(2 * N_EXPERTS):
            @pl.when(used_ref[e] > 0)
            def _():
                zero_copy(e).start()
        for e in range(2 * N_EXPERTS):
            @pl.when(used_ref[e] > 0)
            def _():
                zero_copy(e).wait()

    base = i * tm * TOP_K

    def copy(r, k):
        s = slot_ref[base + r * TOP_K + k]
        return pltpu.make_async_copy(h_ref.at[pl.ds(r, 1)], xs_out.at[pl.ds(s, 1)], sem)

    for r in range(tm):
        for k in range(TOP_K):
            copy(r, k).start(priority=k % 2)
    for r in range(tm):
        for k in range(TOP_K):
            copy(r, k).wait()


def _dispatch(slots, last_rows, used, h, n_rows, tm):
    n = h.shape[0]
    grid_spec = pltpu.PrefetchScalarGridSpec(
        num_scalar_prefetch=3,
        grid=(n // tm,),
        in_specs=[pl.BlockSpec((tm, D_MODEL), lambda i, *_: (i, 0))],
        out_specs=pl.BlockSpec(memory_space=pl.ANY),
        scratch_shapes=[pltpu.VMEM((EXPERT_TILE, D_MODEL), F32),
                        pltpu.SemaphoreType.DMA(()), pltpu.SemaphoreType.DMA(())])
    return pl.pallas_call(
        functools.partial(_dispatch_body, tm=tm),
        grid_spec=grid_spec,
        out_shape=jax.ShapeDtypeStruct((n_rows, D_MODEL), F32),
        compiler_params=_params(("arbitrary",), has_side_effects=True, disable_bounds_checks=True),
        name="dispatch",
    )(slots, last_rows, used, h)


def _expert_body(te_ref, nt_ref, we_ref, xs_ref, w1_ref, b1g_ref, b1l_ref, w2_ref, b2_ref, y_ref,
                 wt, w1g, w1l, w2s):
    del we_ref
    i = pl.program_id(0)
    d_exp = w2_ref.shape[1]
    changed = jnp.logical_or(i == 0, te_ref[i] != te_ref[jnp.maximum(i - 1, 0)])

    @pl.when(jnp.logical_and(changed, i < nt_ref[0]))
    def _():
        chunk = 2 * LANES
        for c in range(2 * d_exp // chunk):
            t = w1_ref[0, :, chunk * c:chunk * (c + 1)].T
            for cb in range(D_MODEL // LANES):
                wt[cb, chunk * c:chunk * (c + 1), :] = t[:, LANES * cb:LANES * (cb + 1)]
        for cb in range(D_MODEL // LANES):
            cols = slice(LANES * cb, LANES * (cb + 1))
            w1g[:, cols] = wt[cb, pl.ds(0, d_exp, stride=2), :].astype(BF16)
            w1l[:, cols] = wt[cb, pl.ds(1, d_exp, stride=2), :].astype(BF16)
        w2s[...] = w2_ref[0].astype(BF16)

    @pl.when(i < nt_ref[0])
    def _():
        x = xs_ref[...].astype(BF16)
        glu = jnp.minimum(_dot_nt(x, w1g[...]) + b1g_ref[0], SWIGLU_LIMIT)
        lin = jnp.clip(_dot_nt(x, w1l[...]) + b1l_ref[0], -SWIGLU_LIMIT, SWIGLU_LIMIT)
        act = glu * jax.nn.sigmoid(SWIGLU_ALPHA * glu) * (lin + 1.0)
        y_ref[...] = _dot(act.astype(BF16), w2s[...]) + b2_ref[0]

    @pl.when(i >= nt_ref[0])
    def _():
        y_ref[...] = jnp.zeros_like(y_ref)


def _experts(tile_expert, n_tiles, weight_expert, xs, w1, b1g, b1l, w2, b2):
    nr = xs.shape[0]
    d_exp = w2.shape[1]
    tg = EXPERT_TILE
    live = lambda i, te, nt, we: (jnp.minimum(i, nt[0] - 1), 0)
    wspec = lambda a, b: pl.BlockSpec((1, a, b), lambda i, te, nt, we: (we[i], 0, 0))
    bspec = lambda b: pl.BlockSpec((1, 1, b), lambda i, te, nt, we: (te[i], 0, 0))
    grid_spec = pltpu.PrefetchScalarGridSpec(
        num_scalar_prefetch=3,
        grid=(nr // tg,),
        in_specs=[pl.BlockSpec((tg, D_MODEL), live),
                  wspec(D_MODEL, 2 * d_exp), bspec(d_exp), bspec(d_exp),
                  wspec(d_exp, D_MODEL), bspec(D_MODEL)],
        out_specs=pl.BlockSpec((tg, D_MODEL), lambda i, te, nt, we: (i, 0)),
        scratch_shapes=[pltpu.VMEM((D_MODEL // LANES, 2 * d_exp, LANES), F32),
                        pltpu.VMEM((d_exp, D_MODEL), BF16),
                        pltpu.VMEM((d_exp, D_MODEL), BF16),
                        pltpu.VMEM((d_exp, D_MODEL), BF16)])
    return pl.pallas_call(
        _expert_body,
        grid_spec=grid_spec,
        out_shape=jax.ShapeDtypeStruct((nr, D_MODEL), F32),
        compiler_params=_params(("arbitrary",)),
        name="experts",
    )(tile_expert, n_tiles, weight_expert, xs, w1, b1g, b1l, w2, b2)


def _combine_body(slot_ref, xp_ref, xs_ref, gate_ref, pp_ref, ps_ref, np_ref, wg_ref, wp_ref,
                  nfin_ref, ys_hbm, op_ref, os_ref, buf, sem, *, tm, npt):
    i = pl.program_id(0)
    n = pl.num_programs(0)

    def copy(ii, sl, r, k):
        s = slot_ref[ii * tm * TOP_K + r * TOP_K + k]
        return pltpu.make_async_copy(ys_hbm.at[pl.ds(s, 1)], buf.at[sl, k, pl.ds(r, 1)], sem.at[sl])

    def issue(ii, sl):
        for r in range(tm):
            for k in range(TOP_K):
                copy(ii, sl, r, k).start(priority=k % 2)

    @pl.when(i == 0)
    def _():
        issue(0, 0)

    sl = lax.rem(i, 2)
    for parity in (0, 1):
        @pl.when(sl == parity)
        def _():
            @pl.when(i + 1 < n)
            def _():
                issue(i + 1, 1 - parity)

            for r in range(tm):
                for k in range(TOP_K):
                    copy(i, parity, r, k).wait()

    g = gate_ref[...]
    x2 = _pick(i, npt, xp_ref, xs_ref)
    for k in range(TOP_K):
        x2 = x2 + g[:, k:k + 1] * buf[sl, k]
    hn = _rms(x2, np_ref[...]).astype(BF16)
    emb = _dot(_pick(i, npt, pp_ref, ps_ref).astype(BF16), wp_ref[...])
    x3 = x2 + jax.nn.sigmoid(_dot(hn, wg_ref[...])) * emb
    _put(i, npt, op_ref, os_ref, _rms(x3, nfin_ref[...]))


def _combine(slots, x1p, x1s, gate, pp, ps, npl, wg, wp, nfin, ys, tm):
    n_prompt, n_sample = x1p.shape[0], x1s.shape[0]
    n = n_prompt + n_sample
    npt = n_prompt // tm
    ple = pp.shape[1]
    vec = pl.BlockSpec((1, D_MODEL), lambda i, s: (0, 0))
    grid_spec = pltpu.PrefetchScalarGridSpec(
        num_scalar_prefetch=1,
        grid=(n // tm,),
        in_specs=[*_split_specs(tm, D_MODEL, npt), pl.BlockSpec((tm, LANES), lambda i, s: (i, 0)),
                  *_split_specs(tm, ple, npt), vec,
                  pl.BlockSpec((D_MODEL, D_MODEL), lambda i, s: (0, 0)),
                  pl.BlockSpec((ple, D_MODEL), lambda i, s: (0, 0)), vec,
                  pl.BlockSpec(memory_space=pl.ANY)],
        out_specs=list(_split_specs(tm, D_MODEL, npt)),
        scratch_shapes=[pltpu.VMEM((2, TOP_K, tm, D_MODEL), F32), pltpu.SemaphoreType.DMA((2,))])
    return pl.pallas_call(
        functools.partial(_combine_body, tm=tm, npt=npt),
        grid_spec=grid_spec,
        out_shape=[jax.ShapeDtypeStruct((n_prompt, D_MODEL), F32),
                   jax.ShapeDtypeStruct((n_sample, D_MODEL), F32)],
        compiler_params=_params(("arbitrary",), disable_bounds_checks=True),
        name="combine",
    )(slots, x1p, x1s, gate, pp, ps, npl, wg, wp, nfin, ys)


def _token_tile(n_prompt, n_sample):
    for tm in (256, 128, 64, 32, 16, 8):
        if n_prompt % tm == 0 and n_sample % tm == 0:
            return tm
    raise ValueError("token counts must be multiples of 8")


def _layer(xp, xs, csp, css, pp, ps, cache_ckv, cache_kpe_t, state_conv, page_table, dims,
           norm_mix, w_in, w_dw, b_dw, ln_g, ln_b, w_pw2, b_pw2, norm_q, w_qb, norm_kv, w_kvb, w_o,
           norm_ffn, w_router, b_router, w_e1, b_e1, w_e2, b_e2, norm_ple, w_ple_gate, w_ple_proj,
           norm_out):
    batch, seq, dec_batch, s_len = dims
    n_prompt, n_sample = batch * seq, dec_batch * s_len
    n = n_prompt + n_sample
    tm = _token_tile(n_prompt, n_sample)
    vec = lambda a: a.reshape(1, -1)

    kpe_end = 2 * D_MODEL + Q_LORA + KV_LORA + QK_ROPE
    win = jnp.concatenate([w_in[:, :kpe_end], jnp.zeros((D_MODEL, _C_ZC - _C_KPE - QK_ROPE), F32),
                           w_in[:, kpe_end:]], axis=1).astype(BF16)
    wq3 = w_qb.reshape(Q_LORA, N_HEADS, QK_NOPE + QK_ROPE)
    wqb = jnp.concatenate([wq3[:, :, :QK_NOPE].reshape(Q_LORA, -1),
                           wq3[:, :, QK_NOPE:].reshape(Q_LORA, -1)], axis=1).astype(BF16)
    wkv3 = w_kvb.reshape(KV_LORA, N_HEADS, QK_NOPE + V_HEAD)
    wkv = jnp.concatenate([wkv3[:, :, :QK_NOPE].reshape(KV_LORA, -1),
                           wkv3[:, :, QK_NOPE:].reshape(KV_LORA, -1)], axis=1).astype(BF16)
    wuk_t = jnp.transpose(wkv3[:, :, :QK_NOPE], (1, 2, 0)).astype(BF16)
    wuv_h = jnp.transpose(wkv3[:, :, QK_NOPE:], (1, 0, 2)).astype(BF16)
    wdw = jnp.broadcast_to(w_dw[:, None, :], (CONV_W, SUBLANES, D_MODEL))

    v_p, v_s, q_cat, ckv_p, ckv_s, kpe_p, kpe_s, zc, za = _inproj(
        xp, xs, csp, css, vec(norm_mix), win, vec(norm_q), wqb, vec(norm_kv), tm)

    k_cat, v_h = _prompt_kv(ckv_p, kpe_p, wkv, tm)
    tq = 512 if seq % 512 == 0 else seq
    attn_p = _prompt_attention(q_cat, k_cat, v_h, batch, seq, tq)

    qs = _sample_q_absorb(q_cat, wuk_t, n_prompt, n_sample)
    o_lat = _sample_attention(page_table, qs, ckv_s, kpe_s, cache_ckv, cache_kpe_t, s_len)
    attn_s = _sample_o_expand(o_lat, wuv_h)

    conv_w = (wdw, vec(b_dw), vec(ln_g), vec(ln_b), w_pw2.astype(BF16), vec(b_pw2), w_o.astype(BF16))
    tc = 256 if seq % 256 == 0 else seq
    x1_p = _convmerge_prompt(v_p, zc, za, attn_p, xp, *conv_w, batch, seq, tc)
    vext = jnp.concatenate([state_conv, v_s.reshape(dec_batch, s_len, D_MODEL)], axis=1)
    g = max(1, min(dec_batch, 256 // s_len))
    x1_s = _convmerge_sample(vext, zc, za, attn_s, xs, *conv_w, n_prompt, g)

    wr_hi = w_router.astype(BF16)
    wr_lo = (w_router - wr_hi.astype(F32)).astype(BF16)
    padw = jnp.zeros((D_MODEL, LANES - N_EXPERTS), BF16)
    wr = jnp.concatenate([wr_hi, padw, wr_lo, padw], axis=1)
    br = jnp.concatenate([b_router, jnp.full((LANES - N_EXPERTS,), NEG, F32)]).reshape(1, LANES)
    h, meta, gate, cnt = _router(x1_p, x1_s, vec(norm_ffn), wr, br, tm)

    tg = EXPERT_TILE
    counts = cnt[0, :N_EXPERTS].astype(jnp.int32)
    tiles_per = (counts + tg - 1) // tg
    tile_end = jnp.cumsum(tiles_per)
    starts = (tile_end - tiles_per) * tg
    n_tiles_max = (n * TOP_K + N_EXPERTS * (tg - 1)) // tg
    picked = meta[:, :TOP_K, None] == jnp.arange(N_EXPERTS, dtype=jnp.int32)
    slots = (jnp.sum(jnp.where(picked, starts, 0), axis=-1) + meta[:, TOP_K:2 * TOP_K]).reshape(-1)
    tile_ids = jnp.arange(n_tiles_max, dtype=jnp.int32)
    tile_expert = jnp.minimum(jnp.sum((tile_end[None, :] <= tile_ids[:, None]).astype(jnp.int32), axis=1),
                              N_EXPERTS - 1)
    n_tiles = tile_end[-1:].astype(jnp.int32)
    tail = n_tiles[0] + jnp.arange(N_EXPERTS, dtype=jnp.int32)
    clear_rows = jnp.concatenate([jnp.maximum(tile_end - 1, 0), jnp.minimum(tail, n_tiles_max - 1)]) * tg
    clear_used = jnp.concatenate([tiles_per, (tail < n_tiles_max).astype(jnp.int32)])

    xs_rows = _dispatch(slots, clear_rows, clear_used, h, n_tiles_max * tg, tm)
    experts = jnp.arange(N_EXPERTS, dtype=jnp.int32)
    later = jnp.logical_and(experts[None, :] > experts[:, None], tiles_per[None, :] > 0)
    next_expert = jnp.min(jnp.where(later, experts[None, :], N_EXPERTS), axis=1)
    next_expert = jnp.where(next_expert < N_EXPERTS, next_expert, experts)
    first_tile = tile_ids == (tile_end - tiles_per)[tile_expert]
    weight_expert = jnp.where(first_tile, tile_expert, next_expert[tile_expert])
    ys = _experts(tile_expert, n_tiles, weight_expert, xs_rows, w_e1, b_e1[:, None, 0::2], b_e1[:, None, 1::2],
                  w_e2, b_e2[:, None, :])

    y_p, y_s = _combine(slots, x1_p, x1_s, gate, pp, ps, vec(norm_ple), w_ple_gate.astype(BF16),
                        w_ple_proj.astype(BF16), vec(norm_out), ys, tm)
    conv_p = v_p.reshape(batch, seq, D_MODEL)[:, seq - (CONV_W - 1):]
    conv_s = vext[:, s_len:]
    return y_p, y_s, ckv_p, ckv_s, kpe_p, kpe_s, conv_p, conv_s


def kernel(x_prompt, x_sample, cache_ckv, cache_kpe, state_conv, page_table, p_prompt, p_sample, norm_mix, w_in, w_dw, b_dw, ln_conv_g, ln_conv_b, w_pw2, b_pw2, norm_q, w_qb, norm_kv, w_kvb, w_o, norm_ffn, w_router, b_router, w_e1, b_e1, w_e2, b_e2, norm_ple, w_ple_gate, w_ple_proj, norm_final):
    depth = w_in.shape[0]
    batch, seq, _ = x_prompt.shape
    dec_batch, s_len, _ = x_sample.shape
    n_prompt, n_sample = batch * seq, dec_batch * s_len
    assert depth == 1 and n_prompt % n_sample == 0 and seq >= HALO and s_len <= SUBLANES
    tm = _token_tile(n_prompt, n_sample)
    assert seq % tm == 0 and tm % s_len == 0
    past_len = page_table.shape[1] * cache_ckv.shape[2]

    half = QK_ROPE // 2
    inv_freq = ROPE_THETA ** (-jnp.arange(half, dtype=F32) / half)
    pos = jnp.concatenate([jnp.arange(seq), past_len + jnp.tile(jnp.arange(s_len), tm // s_len)])
    ang = pos.astype(F32)[:, None] * inv_freq[None, :]
    cos, sin = jnp.cos(ang), jnp.sin(ang)
    cs = jnp.concatenate([cos, cos, cos, cos, -sin, sin, -sin, sin], axis=1)

    i = 0
    y_p, y_s, ckv_p, ckv_s, kpe_p, kpe_s, conv_p, conv_s = _layer(
        x_prompt.reshape(n_prompt, D_MODEL), x_sample.reshape(n_sample, D_MODEL), cs[:seq], cs[seq:],
        p_prompt[i].reshape(n_prompt, -1), p_sample[i].reshape(n_sample, -1),
        cache_ckv[i], jnp.swapaxes(cache_kpe[i], 1, 2), state_conv[i], page_table,
        (batch, seq, dec_batch, s_len),
        norm_mix[i], w_in[i], w_dw[i], b_dw[i], ln_conv_g[i], ln_conv_b[i], w_pw2[i], b_pw2[i],
        norm_q[i], w_qb[i], norm_kv[i], w_kvb[i], w_o[i], norm_ffn[i], w_router[i], b_router[i],
        w_e1[i], b_e1[i], w_e2[i], b_e2[i], norm_ple[i], w_ple_gate[i], w_ple_proj[i], norm_final)
    return (y_p.reshape(batch, seq, D_MODEL), y_s.reshape(dec_batch, s_len, D_MODEL),
            ckv_p.reshape(1, batch, seq, KV_LORA), kpe_p.reshape(1, batch, seq, QK_ROPE), conv_p[None],
            ckv_s.reshape(1, dec_batch, s_len, KV_LORA), kpe_s.reshape(1, dec_batch, s_len, QK_ROPE),
            conv_s[None])
```
